```python
import math
import jax, jax.numpy as jnp
from jax import lax
import numpy as np

D_MODEL = 1024
BATCH = 2
SEQ = 8192
DEPTH = 4

D_FF = ((8 * D_MODEL // 3 + 255) // 256) * 256
S5_WIDTH = D_MODEL // 4
S5_GROUP_DIM = 16
S5_GROUPS = S5_WIDTH // S5_GROUP_DIM
S5_STATE = 64
RET_WIDTH = D_MODEL // 2
RET_HEAD_DIM = 128
RET_HEADS = RET_WIDTH // RET_HEAD_DIM
RET_CHUNK = 128
LRU_WIDTH = D_MODEL - S5_WIDTH - RET_WIDTH
LRU_BLOCKS = 4
LRU_BLOCK_DIM = LRU_WIDTH // LRU_BLOCKS
CONV_WIDTH = 4
LRU_C = 8.0
ROPE_BASE = 10000.0
NORM_EPS = 1e-6
IN_SECTIONS = (S5_WIDTH, RET_WIDTH, RET_WIDTH, RET_WIDTH, RET_WIDTH, LRU_WIDTH, LRU_WIDTH)
IN_WIDTH = sum(IN_SECTIONS)

kernel_name = 'hymba_s5_retnet_rglru_macaron'


def rmsnorm(x, g):
    xf = x.astype(jnp.float32)
    xf = xf * lax.rsqrt(jnp.mean(xf * xf, axis=-1, keepdims=True) + NORM_EPS)
    return xf.astype(x.dtype) * g


def swiglu(x, w_gate, w_up, w_down):
    return (jax.nn.silu(x @ w_gate) * (x @ w_up)) @ w_down


def s5_mixer(u, lam_re, lam_im, log_step, b_re, b_im, c_re, c_im, d_skip, w_glu, b_glu):
    B, S, W = u.shape
    ug = u.reshape(B, S, S5_GROUPS, S5_GROUP_DIM)
    step = jnp.exp(log_step)[:, None]
    ar = lam_re * step
    ai = lam_im * step
    mag = jnp.exp(ar)
    lb_re = mag * jnp.cos(ai)
    lb_im = mag * jnp.sin(ai)
    nr = lb_re - 1.0
    den = lam_re * lam_re + lam_im * lam_im
    fr = (nr * lam_re + lb_im * lam_im) / den
    fi = (lb_im * lam_re - nr * lam_im) / den
    bb_re = fr[..., None] * b_re - fi[..., None] * b_im
    bb_im = fr[..., None] * b_im + fi[..., None] * b_re
    bu_re = jnp.einsum('bsgh,gph->bsgp', ug, bb_re)
    bu_im = jnp.einsum('bsgh,gph->bsgp', ug, bb_im)
    a_re = jnp.broadcast_to(lb_re, bu_re.shape)
    a_im = jnp.broadcast_to(lb_im, bu_im.shape)

    def combine(e1, e2):
        a1r, a1i, b1r, b1i = e1
        a2r, a2i, b2r, b2i = e2
        return (a2r * a1r - a2i * a1i,
                a2r * a1i + a2i * a1r,
                a2r * b1r - a2i * b1i + b2r,
                a2r * b1i + a2i * b1r + b2i)

    _, _, s_re, s_im = lax.associative_scan(combine, (a_re, a_im, bu_re, bu_im), axis=1)
    y = jnp.einsum('bsgp,ghp->bsgh', s_re, c_re) - jnp.einsum('bsgp,ghp->bsgh', s_im, c_im)
    y = y.reshape(B, S, W) + d_skip * u
    y = jax.nn.gelu(y)
    return y * jax.nn.sigmoid(y @ w_glu + b_glu)


def retention(q, k, v, g):
    B, S, _ = q.shape
    H, Dh, C = RET_HEADS, RET_HEAD_DIM, RET_CHUNK
    N = S // C
    dt = q.dtype
    pos = jnp.arange(S, dtype=jnp.float32)
    inv_freq = ROPE_BASE ** (-jnp.arange(0, Dh, 2, dtype=jnp.float32) / Dh)
    ang = pos[:, None] * inv_freq[None, :]
    cos = jnp.cos(ang).astype(dt)[None, :, None, :]
    sin = jnp.sin(ang).astype(dt)[None, :, None, :]

    def rope(t):
        t = t.reshape(B, S, H, Dh)
        t1, t2 = t[..., :Dh // 2], t[..., Dh // 2:]
        return jnp.concatenate([t1 * cos - t2 * sin, t1 * sin + t2 * cos], axis=-1)

    def chunks(t):
        return t.reshape(B, N, C, H, Dh).transpose(0, 3, 1, 2, 4)

    qc = chunks(rope(q) * (Dh ** -0.5))
    kc = chunks(rope(k))
    vc = chunks(v.reshape(B, S, H, Dh))

    log_gamma = jnp.log1p(-jnp.exp2(-5.0 - jnp.arange(H, dtype=jnp.float32)))
    idx = jnp.arange(C, dtype=jnp.float32)
    diff = idx[:, None] - idx[None, :]
    decay = jnp.where(diff[None] >= 0,
                      jnp.exp(jnp.maximum(diff, 0.0)[None] * log_gamma[:, None, None]),
                      0.0).astype(dt)
    zeta = jnp.exp((C - 1.0 - idx)[None] * log_gamma[:, None]).astype(dt)
    xi = jnp.exp((idx + 1.0)[None] * log_gamma[:, None]).astype(dt)
    gamma_chunk = jnp.exp(C * log_gamma).astype(dt)

    scores = jnp.einsum('bhncd,bhnmd->bhncm', qc, kc) * decay[None, :, None]
    inner = jnp.einsum('bhncm,bhnme->bhnce', scores, vc)
    kv = jnp.einsum('bhnmd,bhnme->bhnde', kc * zeta[None, :, None, :, None], vc)

    def step(state, kv_n):
        return gamma_chunk[None, :, None, None] * state + kv_n, state

    _, prev = lax.scan(step, jnp.zeros((B, H, Dh, Dh), dt), jnp.moveaxis(kv, 2, 0))
    prev = jnp.moveaxis(prev, 0, 2)
    cross = jnp.einsum('bhncd,bhnde->bhnce', qc, prev) * xi[None, :, None, :, None]
    o = (inner + cross).transpose(0, 2, 3, 1, 4).reshape(B, S, H, Dh)
    of = o.astype(jnp.float32)
    of = of * lax.rsqrt(jnp.mean(of * of, axis=-1, keepdims=True) + NORM_EPS)
    o = of.astype(dt).reshape(B, S, H * Dh)
    return jax.nn.silu(g) * o


def rglru_branch(xb, gate_b, conv_w, conv_b, w_a, b_a, w_x, b_x, lam):
    B, S, W = xb.shape
    xc = lax.conv_general_dilated(xb, conv_w.reshape(CONV_WIDTH, 1, W), window_strides=(1,),
                                  padding=[(CONV_WIDTH - 1, 0)],
                                  dimension_numbers=('NWC', 'WIO', 'NWC'),
                                  feature_group_count=W) + conv_b
    xblk = xc.reshape(B, S, LRU_BLOCKS, LRU_BLOCK_DIM)
    r = jax.nn.sigmoid(jnp.einsum('bsnd,nde->bsne', xblk, w_a) + b_a.reshape(LRU_BLOCKS, LRU_BLOCK_DIM)).reshape(B, S, W)
    i = jax.nn.sigmoid(jnp.einsum('bsnd,nde->bsne', xblk, w_x) + b_x.reshape(LRU_BLOCKS, LRU_BLOCK_DIM)).reshape(B, S, W)
    log_a = -LRU_C * r * jax.nn.softplus(-lam)
    a = jnp.exp(log_a)
    mult = jnp.sqrt(-jnp.expm1(2.0 * log_a))
    bterm = mult * (i * xc)

    def combine(e1, e2):
        a1, b1 = e1
        a2, b2 = e2
        return a2 * a1, a2 * b1 + b2

    _, h = lax.associative_scan(combine, (a, bterm), axis=1)
    return h * jax.nn.gelu(gate_b)


def setup_inputs(seed: int = 0) -> dict:
    key = jax.random.key(seed)
    ks = iter(jax.random.split(key, 48))
    L, D, F = DEPTH, D_MODEL, D_FF
    G, P, Hg = S5_GROUPS, S5_STATE, S5_GROUP_DIM
    f32 = jnp.float32

    def nrm(shape, scale):
        return jax.random.normal(next(ks), shape, f32) * scale

    def gain(shape):
        return 1.0 + 0.01 * jax.random.normal(next(ks), shape, f32)

    u_lru = jax.random.uniform(next(ks), (L, LRU_WIDTH), f32, 0.9, 0.999)
    a0 = u_lru ** (1.0 / LRU_C)
    return {
        'x': jax.random.normal(next(ks), (BATCH, SEQ, D), f32),
        'ffn1_norm': gain((L, D)),
        'ffn1_w_gate': nrm((L, D, F), D ** -0.5),
        'ffn1_w_up': nrm((L, D, F), D ** -0.5),
        'ffn1_w_down': nrm((L, F, D), F ** -0.5),
        'mix_norm': gain((L, D)),
        'w_in': nrm((L, D, IN_WIDTH), D ** -0.5),
        's5_lambda_re': -0.5 + 0.01 * jax.random.normal(next(ks), (L, G, P), f32),
        's5_lambda_im': jnp.pi * jnp.arange(P, dtype=f32)[None, None, :] + 0.01 * jax.random.normal(next(ks), (L, G, P), f32),
        's5_log_step': jax.random.uniform(next(ks), (L, G), f32, math.log(1e-3), math.log(1e-1)),
        's5_b_re': nrm((L, G, P, Hg), (2.0 * Hg) ** -0.5),
        's5_b_im': nrm((L, G, P, Hg), (2.0 * Hg) ** -0.5),
        's5_c_re': nrm((L, G, Hg, P), P ** -0.5),
        's5_c_im': nrm((L, G, Hg, P), P ** -0.5),
        's5_d': nrm((L, S5_WIDTH), 1.0),
        's5_w_glu': nrm((L, S5_WIDTH, S5_WIDTH), S5_WIDTH ** -0.5),
        's5_b_glu': nrm((L, S5_WIDTH), 0.01),
        's5_out_norm': gain((L, S5_WIDTH)),
        'ret_out_norm': gain((L, RET_WIDTH)),
        'lru_conv_w': nrm((L, CONV_WIDTH, LRU_WIDTH), CONV_WIDTH ** -0.5),
        'lru_conv_b': nrm((L, LRU_WIDTH), 0.01),
        'lru_w_a': nrm((L, LRU_BLOCKS, LRU_BLOCK_DIM, LRU_BLOCK_DIM), LRU_BLOCK_DIM ** -0.5),
        'lru_b_a': nrm((L, LRU_WIDTH), 0.01),
        'lru_w_x': nrm((L, LRU_BLOCKS, LRU_BLOCK_DIM, LRU_BLOCK_DIM), LRU_BLOCK_DIM ** -0.5),
        'lru_b_x': nrm((L, LRU_WIDTH), 0.01),
        'lru_lambda': jnp.log(a0) - jnp.log1p(-a0),
        'lru_out_norm': gain((L, LRU_WIDTH)),
        'w_out': nrm((L, D, D), D ** -0.5),
        'ffn2_norm': gain((L, D)),
        'ffn2_w_gate': nrm((L, D, F), D ** -0.5),
        'ffn2_w_up': nrm((L, D, F), D ** -0.5),
        'ffn2_w_down': nrm((L, F, D), F ** -0.5),
        'final_norm': gain((D,)),
    }


def reference(x, ffn1_norm, ffn1_w_gate, ffn1_w_up, ffn1_w_down, mix_norm, w_in,
              s5_lambda_re, s5_lambda_im, s5_log_step, s5_b_re, s5_b_im, s5_c_re, s5_c_im,
              s5_d, s5_w_glu, s5_b_glu, s5_out_norm, ret_out_norm,
              lru_conv_w, lru_conv_b, lru_w_a, lru_b_a, lru_w_x, lru_b_x, lru_lambda, lru_out_norm,
              w_out, ffn2_norm, ffn2_w_gate, ffn2_w_up, ffn2_w_down, final_norm):
    split_pts = [int(p) for p in np.cumsum(IN_SECTIONS)[:-1]]
    for l in range(DEPTH):
        h = rmsnorm(x, ffn1_norm[l])
        x = x + 0.5 * swiglu(h, ffn1_w_gate[l], ffn1_w_up[l], ffn1_w_down[l])
        h = rmsnorm(x, mix_norm[l])
        z = h @ w_in[l]
        u_s5, q, k, v, g, x_lru, g_lru = jnp.split(z, split_pts, axis=-1)
        y_s5 = s5_mixer(u_s5, s5_lambda_re[l], s5_lambda_im[l], s5_log_step[l], s5_b_re[l], s5_b_im[l],
                        s5_c_re[l], s5_c_im[l], s5_d[l], s5_w_glu[l], s5_b_glu[l])
        y_ret = retention(q, k, v, g)
        y_lru = rglru_branch(x_lru, g_lru, lru_conv_w[l], lru_conv_b[l], lru_w_a[l], lru_b_a[l],
                             lru_w_x[l], lru_b_x[l], lru_lambda[l])
        y = jnp.concatenate([rmsnorm(y_s5, s5_out_norm[l]),
                             rmsnorm(y_ret, ret_out_norm[l]),
                             rmsnorm(y_lru, lru_out_norm[l])], axis=-1)
        x = x + y @ w_out[l]
        h = rmsnorm(x, ffn2_norm[l])
        x = x + 0.5 * swiglu(h, ffn2_w_gate[l], ffn2_w_up[l], ffn2_w_down[l])
    return rmsnorm(x, final_norm)
```

```python
import functools
import math

import jax
import jax.numpy as jnp
from jax import lax
from jax.experimental import pallas as pl
from jax.experimental.pallas import tpu as pltpu

F32 = jnp.float32
BF16 = jnp.bfloat16

D_MODEL = 1024
D_FF = 2816
S5_WIDTH = 256
S5_GROUP_DIM = 16
S5_GROUPS = 16
S5_STATE = 64
S5_CH = S5_GROUPS * S5_STATE
RET_WIDTH = 512
RET_HEAD_DIM = 128
RET_HEADS = 4
RET_CHUNK = 128
LRU_WIDTH = 256
LRU_BLOCKS = 4
LRU_BLOCK_DIM = 64
CONV_WIDTH = 4
LRU_C = 8.0
ROPE_BASE = 10000.0
NORM_EPS = 1e-6
IN_SECTIONS = (S5_WIDTH, RET_WIDTH, RET_WIDTH, RET_WIDTH, RET_WIDTH, LRU_WIDTH, LRU_WIDTH)
IN_WIDTH = sum(IN_SECTIONS)

SUBLANES = 8
LANES = 128
MIB = 1024 * 1024

FFN_ROWS = 512
FFN_CHUNK = 256
PROJ_ROWS = 512
S5_ROWS = 256
LRU_ROWS = 256
RET_ROWS = 256
NORM_ROWS = 1024


def _params(vmem_mib, n_axes):
    return pltpu.CompilerParams(
        dimension_semantics=("arbitrary",) * n_axes,
        vmem_limit_bytes=vmem_mib * MIB)


def _rms_rows(x):
    return x * lax.rsqrt(jnp.mean(x * x, axis=-1, keepdims=True) + NORM_EPS)


def _gelu_tanh(x):
    c = math.sqrt(2.0 / math.pi)
    return x * (0.5 * (1.0 + jnp.tanh(c * (x + 0.044715 * (x * x * x)))))


def _dot(a, b):
    return jnp.dot(a, b, preferred_element_type=F32)


def _resident(shape, layer):
    nd = len(shape)
    return pl.BlockSpec((None,) + tuple(shape), lambda *_: (layer,) + (0,) * nd,
                        pipeline_mode=pl.Buffered(1))


def _ffn_kernel(x_ref, g_ref, wg_ref, wu_ref, wd_ref, o_ref):
    x = x_ref[...]
    h = (_rms_rows(x) * g_ref[...]).astype(BF16)
    acc = jnp.zeros(x.shape, F32)
    for c in range(D_FF // FFN_CHUNK):
        sl = slice(c * FFN_CHUNK, (c + 1) * FFN_CHUNK)
        gate = _dot(h, wg_ref[:, sl])
        up = _dot(h, wu_ref[:, sl])
        act = (gate * jax.nn.sigmoid(gate) * up).astype(BF16)
        acc = acc + _dot(act, wd_ref[sl, :])
    o_ref[...] = x + 0.5 * acc


def _ffn(x, gain, wg, wu, wd, layer):
    t = x.shape[0]
    row = pl.BlockSpec((FFN_ROWS, D_MODEL), lambda i: (i, 0))
    return pl.pallas_call(
        _ffn_kernel,
        grid=(t // FFN_ROWS,),
        in_specs=[row,
                  _resident((1, D_MODEL), layer),
                  _resident((D_MODEL, D_FF), layer),
                  _resident((D_MODEL, D_FF), layer),
                  _resident((D_FF, D_MODEL), layer)],
        out_specs=row,
        out_shape=jax.ShapeDtypeStruct(x.shape, F32),
        compiler_params=_params(48, 1),
        name="ffn",
    )(x, gain, wg, wu, wd)


def _inproj_kernel(x_ref, g_ref, w_ref, *out_refs):
    h = (_rms_rows(x_ref[...]) * g_ref[...]).astype(BF16)
    off = 0
    for o_ref, width in zip(out_refs, IN_SECTIONS):
        o_ref[...] = _dot(h, w_ref[:, off:off + width])
        off += width


def _inproj(x, gain, w_in, layer):
    t = x.shape[0]
    return pl.pallas_call(
        _inproj_kernel,
        grid=(t // PROJ_ROWS,),
        in_specs=[pl.BlockSpec((PROJ_ROWS, D_MODEL), lambda i: (i, 0)),
                  _resident((1, D_MODEL), layer),
                  _resident((D_MODEL, IN_WIDTH), layer)],
        out_specs=[pl.BlockSpec((PROJ_ROWS, w), lambda i: (i, 0)) for w in IN_SECTIONS],
        out_shape=[jax.ShapeDtypeStruct((t, w), F32) for w in IN_SECTIONS],
        compiler_params=_params(40, 1),
        name="inproj",
    )(x, gain, w_in)


S5_CONST_ROWS = 8 * SUBLANES


def _cmul(a, b):
    return a[0] * b[0] - a[1] * b[1], a[0] * b[1] + a[1] * b[0]


def _s5_disc_kernel(lr_ref, li_ref, ls_ref, er_ref, ei_ref, b_ref, c_ref):
    lr = lr_ref[...]
    li = li_ref[...]
    step = jnp.exp(ls_ref[...])
    ar = lr * step
    ai = li * step
    mag = jnp.exp(ar)
    p1 = (mag * jnp.cos(ai), mag * jnp.sin(ai))
    nr = p1[0] - 1.0
    den = lr * lr + li * li
    fr = (nr * lr + p1[1] * li) / den
    fi = (p1[1] * lr - nr * li) / den
    er = er_ref[...]
    ei = ei_ref[...]
    b_ref[:, :S5_CH] = (fr * er - fi * ei).astype(BF16)
    b_ref[:, S5_CH:] = (fr * ei + fi * er).astype(BF16)
    p2 = _cmul(p1, p1)
    p3 = _cmul(p2, p1)
    p4 = _cmul(p2, p2)
    powers = [p1, p2, p3, p4, _cmul(p4, p1), _cmul(p4, p2), _cmul(p4, p3), _cmul(p4, p4)]
    row = lax.broadcasted_iota(jnp.int32, (SUBLANES, S5_CH), 0)
    zero = jnp.zeros((SUBLANES, S5_CH), F32)
    for s, (d, p) in enumerate(((1, p1), (2, p2), (4, p4))):
        for part in range(2):
            base = (2 * s + part) * SUBLANES
            c_ref[base:base + SUBLANES, :] = jnp.where(
                row >= d, jnp.broadcast_to(p[part], (SUBLANES, S5_CH)), zero)
    for part in range(2):
        tile = zero
        for r in range(SUBLANES):
            tile = jnp.where(row == r, jnp.broadcast_to(powers[r][part], (SUBLANES, S5_CH)), tile)
        base = (6 + part) * SUBLANES
        c_ref[base:base + SUBLANES, :] = tile


def _s5_discretise(lam_re, lam_im, log_step, b_re, b_im):
    depth = lam_re.shape[0]
    eye = jnp.eye(S5_GROUPS, dtype=F32)

    def expand(b):
        return jnp.einsum('lgph,gk->lghkp', b, eye).reshape(depth, S5_WIDTH, S5_CH)

    vec = lambda a: a.reshape(depth, 1, S5_CH)
    ls = jnp.broadcast_to(log_step[:, :, None], (depth, S5_GROUPS, S5_STATE))
    vspec = pl.BlockSpec((None, 1, S5_CH), lambda l: (l, 0, 0))
    mspec = pl.BlockSpec((None, S5_WIDTH, S5_CH), lambda l: (l, 0, 0))
    return pl.pallas_call(
        _s5_disc_kernel,
        grid=(depth,),
        in_specs=[vspec, vspec, vspec, mspec, mspec],
        out_specs=[pl.BlockSpec((None, S5_WIDTH, 2 * S5_CH), lambda l: (l, 0, 0)),
                   pl.BlockSpec((None, S5_CONST_ROWS, S5_CH), lambda l: (l, 0, 0))],
        out_shape=[jax.ShapeDtypeStruct((depth, S5_WIDTH, 2 * S5_CH), BF16),
                   jax.ShapeDtypeStruct((depth, S5_CONST_ROWS, S5_CH), F32)],
        compiler_params=_params(32, 1),
        name="s5_disc",
    )(vec(lam_re), vec(lam_im), vec(ls), expand(b_re), expand(b_im))


def _s5_kernel(u_ref, b_ref, c_ref, k_ref, d_ref, wglu_ref, bglu_ref, gain_ref, o_ref,
               bu_scr, s_scr):
    j = pl.program_id(1)
    rows = u_ref.shape[0]

    @pl.when(j == 0)
    def _():
        s_scr[0:SUBLANES, :] = jnp.zeros((SUBLANES, 2 * S5_CH), F32)

    @pl.when(j > 0)
    def _():
        s_scr[0:SUBLANES, :] = s_scr[rows:rows + SUBLANES, :]

    u = u_ref[...]
    bu_scr[...] = _dot(u.astype(BF16), b_ref[...])

    def tile_body(i, carry):
        off = pl.multiple_of(i * SUBLANES, SUBLANES)
        for cb in range(S5_CH // LANES):
            re = slice(cb * LANES, (cb + 1) * LANES)
            im = slice(S5_CH + cb * LANES, S5_CH + (cb + 1) * LANES)
            br = bu_scr[pl.ds(off, SUBLANES), re]
            bi = bu_scr[pl.ds(off, SUBLANES), im]
            for s, d in enumerate((1, 2, 4)):
                cr = k_ref[(2 * s) * SUBLANES:(2 * s + 1) * SUBLANES, re]
                ci = k_ref[(2 * s + 1) * SUBLANES:(2 * s + 2) * SUBLANES, re]
                rr = pltpu.roll(br, d, 0)
                ri = pltpu.roll(bi, d, 0)
                br, bi = br + (cr * rr - ci * ri), bi + (cr * ri + ci * rr)
            last = slice(SUBLANES - 1, SUBLANES)
            pr = jnp.broadcast_to(s_scr[pl.ds(off, SUBLANES), re][last, :], (SUBLANES, LANES))
            pi = jnp.broadcast_to(s_scr[pl.ds(off, SUBLANES), im][last, :], (SUBLANES, LANES))
            wr = k_ref[6 * SUBLANES:7 * SUBLANES, re]
            wi = k_ref[7 * SUBLANES:8 * SUBLANES, re]
            s_scr[pl.ds(off + SUBLANES, SUBLANES), re] = br + (wr * pr - wi * pi)
            s_scr[pl.ds(off + SUBLANES, SUBLANES), im] = bi + (wr * pi + wi * pr)
        return carry

    lax.fori_loop(0, rows // SUBLANES, tile_body, 0)

    state = s_scr[SUBLANES:rows + SUBLANES, :].astype(BF16)
    y = _dot(state, c_ref[...]) + d_ref[...] * u
    y = _gelu_tanh(y)
    y = y * jax.nn.sigmoid(_dot(y.astype(BF16), wglu_ref[...]) + bglu_ref[...])
    o_ref[...] = (_rms_rows(y) * gain_ref[...]).astype(BF16)


def _s5(u, bblk, cblk, consts, d_skip, w_glu, b_glu, gain, layer, batch):
    t = u.shape[0]
    nblk = t // batch // S5_ROWS
    row = pl.BlockSpec((S5_ROWS, S5_WIDTH), lambda b, j: (b * nblk + j, 0))
    return pl.pallas_call(
        _s5_kernel,
        grid=(batch, nblk),
        in_specs=[row,
                  _resident((S5_WIDTH, 2 * S5_CH), layer),
                  _resident((2 * S5_CH, S5_WIDTH), layer),
                  _resident((S5_CONST_ROWS, S5_CH), layer),
                  _resident((1, S5_WIDTH), layer),
                  _resident((S5_WIDTH, S5_WIDTH), layer),
                  _resident((1, S5_WIDTH), layer),
                  _resident((1, S5_WIDTH), layer)],
        out_specs=row,
        out_shape=jax.ShapeDtypeStruct((t, S5_WIDTH), BF16),
        scratch_shapes=[pltpu.VMEM((S5_ROWS, 2 * S5_CH), F32),
                        pltpu.VMEM((S5_ROWS + SUBLANES, 2 * S5_CH), F32)],
        compiler_params=_params(32, 2),
        name="s5",
    )(u, bblk, cblk, consts, d_skip, w_glu, b_glu, gain)


def _lru_kernel(x_ref, gate_ref, cw_ref, cb_ref, wax_ref, bax_ref, lam_ref, gain_ref, o_ref,
                ext_scr, a_scr, b_scr, h_scr):
    j = pl.program_id(1)
    rows = x_ref.shape[0]
    pad = SUBLANES

    @pl.when(j == 0)
    def _():
        ext_scr[0:pad, :] = jnp.zeros((pad, LRU_WIDTH), F32)
        h_scr[0:pad, :] = jnp.zeros((pad, LRU_WIDTH), F32)

    @pl.when(j > 0)
    def _():
        ext_scr[0:pad, :] = ext_scr[rows:rows + pad, :]
        h_scr[0:pad, :] = h_scr[rows:rows + pad, :]

    ext_scr[pad:rows + pad, :] = x_ref[...]
    xc = cb_ref[...] + cw_ref[CONV_WIDTH - 1:CONV_WIDTH, :] * ext_scr[pad:rows + pad, :]
    for lag in range(1, CONV_WIDTH):
        tap = cw_ref[CONV_WIDTH - 1 - lag:CONV_WIDTH - lag, :]
        xc = xc + tap * ext_scr[pad - lag:rows + pad - lag, :]

    gates = jax.nn.sigmoid(_dot(xc.astype(BF16), wax_ref[...]) + bax_ref[...])
    r = gates[:, :LRU_WIDTH]
    i = gates[:, LRU_WIDTH:]
    z = -lam_ref[...]
    softplus = jnp.maximum(z, 0.0) + jnp.log1p(jnp.exp(-jnp.abs(z)))
    log_a = (-LRU_C) * r * softplus
    a = jnp.exp(log_a)
    a_scr[...] = a
    b_scr[...] = jnp.sqrt(1.0 - jnp.exp(2.0 * log_a)) * (i * xc)

    row = lax.broadcasted_iota(jnp.int32, (SUBLANES, LRU_WIDTH), 0)

    def tile_body(t, carry):
        off = pl.multiple_of(t * SUBLANES, SUBLANES)
        at = a_scr[pl.ds(off, SUBLANES), :]
        bt = b_scr[pl.ds(off, SUBLANES), :]
        for d in (1, 2, 4):
            keep = row >= d
            ar = jnp.where(keep, pltpu.roll(at, d, 0), 1.0)
            br = jnp.where(keep, pltpu.roll(bt, d, 0), 0.0)
            bt = at * br + bt
            at = at * ar
        prev = jnp.broadcast_to(h_scr[pl.ds(off, SUBLANES), :][SUBLANES - 1:SUBLANES, :],
                                (SUBLANES, LRU_WIDTH))
        h_scr[pl.ds(off + SUBLANES, SUBLANES), :] = at * prev + bt
        return carry

    lax.fori_loop(0, rows // SUBLANES, tile_body, 0)

    y = h_scr[pad:rows + pad, :] * _gelu_tanh(gate_ref[...])
    o_ref[...] = (_rms_rows(y) * gain_ref[...]).astype(BF16)


def _lru(x, gate, conv_w, conv_b, wax, bax, lam, gain, layer, batch):
    t = x.shape[0]
    nblk = t // batch // LRU_ROWS
    row = pl.BlockSpec((LRU_ROWS, LRU_WIDTH), lambda b, j: (b * nblk + j, 0))
    return pl.pallas_call(
        _lru_kernel,
        grid=(batch, nblk),
        in_specs=[row, row,
                  _resident((CONV_WIDTH, LRU_WIDTH), layer),
                  _resident((1, LRU_WIDTH), layer),
                  _resident((LRU_WIDTH, 2 * LRU_WIDTH), layer),
                  _resident((1, 2 * LRU_WIDTH), layer),
                  _resident((1, LRU_WIDTH), layer),
                  _resident((1, LRU_WIDTH), layer)],
        out_specs=row,
        out_shape=jax.ShapeDtypeStruct((t, LRU_WIDTH), BF16),
        scratch_shapes=[pltpu.VMEM((LRU_ROWS + SUBLANES, LRU_WIDTH), F32),
                        pltpu.VMEM((LRU_ROWS, LRU_WIDTH), F32),
                        pltpu.VMEM((LRU_ROWS, LRU_WIDTH), F32),
                        pltpu.VMEM((LRU_ROWS + SUBLANES, LRU_WIDTH), F32)],
        compiler_params=_params(32, 2),
        name="lru",
    )(x, gate, conv_w, conv_b, wax, bax, lam, gain)


def _ret_kernel(q_ref, k_ref, v_ref, g_ref, cos_ref, sin_ref, decay_ref, zeta_ref, xi_ref,
                gamma_ref, gain_ref, o_ref, state_scr, y_scr):
    j = pl.program_id(1)

    @pl.when(j == 0)
    def _():
        state_scr[...] = jnp.zeros(state_scr.shape, F32)

    scale = RET_HEAD_DIM ** -0.5
    half = RET_HEAD_DIM // 2
    for c in range(q_ref.shape[0] // RET_CHUNK):
        rows = slice(c * RET_CHUNK, (c + 1) * RET_CHUNK)
        cs = cos_ref[rows, :]
        sn = sin_ref[rows, :]
        ssq = jnp.zeros((RET_CHUNK, 1), F32)
        for h in range(RET_HEADS):
            cols = slice(h * RET_HEAD_DIM, (h + 1) * RET_HEAD_DIM)
            qh = q_ref[rows, cols]
            kh = k_ref[rows, cols]
            vh = v_ref[rows, cols].astype(BF16)
            qr = (qh * cs + pltpu.roll(qh, half, 1) * sn) * scale
            kr = kh * cs + pltpu.roll(kh, half, 1) * sn
            qb = qr.astype(BF16)
            scores = lax.dot_general(qb, kr.astype(BF16), (((1,), (1,)), ((), ())),
                                     preferred_element_type=F32) * decay_ref[:, cols]
            inner = _dot(scores.astype(BF16), vh)
            st = state_scr[:, cols]
            cross = _dot(qb, st.astype(BF16)) * xi_ref[:, cols]
            kz = (kr * zeta_ref[:, cols]).T.astype(BF16)
            state_scr[:, cols] = gamma_ref[:, cols] * st + _dot(kz, vh)
            gh = g_ref[rows, cols]
            yh = (gh * jax.nn.sigmoid(gh)) * _rms_rows(inner + cross)
            ssq = ssq + jnp.sum(yh * yh, axis=-1, keepdims=True)
            y_scr[:, cols] = yh
        inv = lax.rsqrt(ssq * (1.0 / RET_WIDTH) + NORM_EPS)
        o_ref[rows, :] = ((y_scr[...] * inv) * gain_ref[...]).astype(BF16)


def _ret_tables(seq):
    dh, c, nh = RET_HEAD_DIM, RET_CHUNK, RET_HEADS
    pos = jnp.arange(seq, dtype=F32)
    inv_freq = ROPE_BASE ** (-jnp.arange(0, dh, 2, dtype=F32) / dh)
    ang = pos[:, None] * inv_freq[None, :]
    cos = jnp.cos(ang)
    sin = jnp.sin(ang)
    cos2 = jnp.concatenate([cos, cos], axis=-1)
    sin2 = jnp.concatenate([-sin, sin], axis=-1)
    log_gamma = jnp.log1p(-jnp.exp2(-5.0 - jnp.arange(nh, dtype=F32)))
    idx = jnp.arange(c, dtype=F32)
    diff = idx[:, None] - idx[None, :]
    decay = jnp.where(diff[None] >= 0,
                      jnp.exp(jnp.maximum(diff, 0.0)[None] * log_gamma[:, None, None]), 0.0)
    zeta = jnp.exp((c - 1.0 - idx)[None] * log_gamma[:, None])
    xi = jnp.exp((idx + 1.0)[None] * log_gamma[:, None])
    gamma_chunk = jnp.exp(c * log_gamma)
    lanes = lambda a: jnp.broadcast_to(a.T[:, :, None], (c, nh, dh)).reshape(c, nh * dh)
    decay_l = decay.transpose(1, 0, 2).reshape(c, nh * c)
    gamma_l = jnp.broadcast_to(gamma_chunk[:, None], (nh, dh)).reshape(1, nh * dh)
    return cos2, sin2, decay_l, lanes(zeta), lanes(xi), gamma_l


def _ret(q, k, v, g, tables, gain, layer, batch):
    t = q.shape[0]
    nblk = t // batch // RET_ROWS
    cos2, sin2, decay_l, zeta_l, xi_l, gamma_l = tables
    row = pl.BlockSpec((RET_ROWS, RET_WIDTH), lambda b, j: (b * nblk + j, 0))
    rot = pl.BlockSpec((RET_ROWS, RET_HEAD_DIM), lambda b, j: (j, 0))
    const = lambda shape: pl.BlockSpec(shape, lambda b, j: (0, 0), pipeline_mode=pl.Buffered(1))
    return pl.pallas_call(
        _ret_kernel,
        grid=(batch, nblk),
        in_specs=[row, row, row, row, rot, rot,
                  const((RET_CHUNK, RET_WIDTH)), const((RET_CHUNK, RET_WIDTH)),
                  const((RET_CHUNK, RET_WIDTH)), const((1, RET_WIDTH)),
                  _resident((1, RET_WIDTH), layer)],
        out_specs=row,
        out_shape=jax.ShapeDtypeStruct((t, RET_WIDTH), BF16),
        scratch_shapes=[pltpu.VMEM((RET_HEAD_DIM, RET_WIDTH), F32),
                        pltpu.VMEM((RET_CHUNK, RET_WIDTH), F32)],
        compiler_params=_params(32, 2),
        name="ret",
    )(q, k, v, g, cos2, sin2, decay_l, zeta_l, xi_l, gamma_l, gain)


def _outproj_kernel(x_ref, ys_ref, yr_ref, yl_ref, w_ref, o_ref):
    acc = x_ref[...] + _dot(ys_ref[...], w_ref[0:S5_WIDTH, :])
    acc = acc + _dot(yr_ref[...], w_ref[S5_WIDTH:S5_WIDTH + RET_WIDTH, :])
    o_ref[...] = acc + _dot(yl_ref[...], w_ref[S5_WIDTH + RET_WIDTH:, :])


def _outproj(x, y_s5, y_ret, y_lru, w_out, layer):
    t = x.shape[0]
    spec = lambda w: pl.BlockSpec((PROJ_ROWS, w), lambda i: (i, 0))
    return pl.pallas_call(
        _outproj_kernel,
        grid=(t // PROJ_ROWS,),
        in_specs=[spec(D_MODEL), spec(S5_WIDTH), spec(RET_WIDTH), spec(LRU_WIDTH),
                  _resident((D_MODEL, D_MODEL), layer)],
        out_specs=spec(D_MODEL),
        out_shape=jax.ShapeDtypeStruct(x.shape, F32),
        compiler_params=_params(32, 1),
        name="outproj",
    )(x, y_s5, y_ret, y_lru, w_out)


def _norm_kernel(x_ref, g_ref, o_ref):
    o_ref[...] = _rms_rows(x_ref[...]) * g_ref[...]


def _final_norm(x, gain):
    t = x.shape[0]
    row = pl.BlockSpec((NORM_ROWS, D_MODEL), lambda i: (i, 0))
    return pl.pallas_call(
        _norm_kernel,
        grid=(t // NORM_ROWS,),
        in_specs=[row, pl.BlockSpec((1, D_MODEL), lambda i: (0, 0))],
        out_specs=row,
        out_shape=jax.ShapeDtypeStruct(x.shape, F32),
        compiler_params=_params(32, 1),
        name="final_norm",
    )(x, gain)


def _block_diag(w):
    depth, n, d, e = w.shape
    eye = jnp.eye(n, dtype=w.dtype)
    return jnp.einsum('lnde,nm->lndme', w, eye).reshape(depth, n * d, n * e)


def kernel(x, ffn1_norm, ffn1_w_gate, ffn1_w_up, ffn1_w_down, mix_norm, w_in, s5_lambda_re, s5_lambda_im, s5_log_step, s5_b_re, s5_b_im, s5_c_re, s5_c_im, s5_d, s5_w_glu, s5_b_glu, s5_out_norm, ret_out_norm, lru_conv_w, lru_conv_b, lru_w_a, lru_b_a, lru_w_x, lru_b_x, lru_lambda, lru_out_norm, w_out, ffn2_norm, ffn2_w_gate, ffn2_w_up, ffn2_w_down, final_norm):
    batch, seq, d = x.shape
    depth = w_in.shape[0]
    t = batch * seq
    assert d == D_MODEL and seq % max(S5_ROWS, LRU_ROWS, RET_ROWS) == 0 and t % NORM_ROWS == 0

    row3 = lambda a: a.reshape(depth, 1, a.shape[-1])
    bf = lambda a: a.astype(BF16)

    f1 = (row3(ffn1_norm), bf(ffn1_w_gate), bf(ffn1_w_up), bf(ffn1_w_down))
    f2 = (row3(ffn2_norm), bf(ffn2_w_gate), bf(ffn2_w_up), bf(ffn2_w_down))
    w_in_b = bf(w_in)
    w_out_b = bf(w_out)
    eye_g = jnp.eye(S5_GROUPS, dtype=F32)
    c_blk = lambda c: jnp.einsum('lghp,gk->lgpkh', c, eye_g).reshape(depth, S5_CH, S5_WIDTH)
    s5_cblk = bf(jnp.concatenate([c_blk(s5_c_re), -c_blk(s5_c_im)], axis=1))
    s5_bblk, s5_consts = _s5_discretise(s5_lambda_re, s5_lambda_im, s5_log_step, s5_b_re, s5_b_im)
    lru_wax = bf(jnp.concatenate([_block_diag(lru_w_a), _block_diag(lru_w_x)], axis=-1))
    lru_bax = row3(jnp.concatenate([lru_b_a, lru_b_x], axis=-1))
    tables = _ret_tables(seq)

    xt = x.reshape(t, d)
    for l in range(depth):
        xt = _ffn(xt, *f1, l)
        u, q, k, v, g, xl, gl = _inproj(xt, row3(mix_norm), w_in_b, l)
        y_s5 = _s5(u, s5_bblk, s5_cblk, s5_consts, row3(s5_d), bf(s5_w_glu), row3(s5_b_glu),
                   row3(s5_out_norm), l, batch)
        y_ret = _ret(q, k, v, g, tables, row3(ret_out_norm), l, batch)
        y_lru = _lru(xl, gl, lru_conv_w, row3(lru_conv_b), lru_wax, lru_bax, row3(lru_lambda),
                     row3(lru_out_norm), l, batch)
        xt = _outproj(xt, y_s5, y_ret, y_lru, w_out_b, l)
        xt = _ffn(xt, *f2, l)
    return _final_norm(xt, final_norm.reshape(1, d)).reshape(batch, seq, d)
```

```python
import functools
import math

import jax
import jax.numpy as jnp
from jax import lax
from jax.experimental import pallas as pl
from jax.experimental.pallas import tpu as pltpu

F32 = jnp.float32
BF16 = jnp.bfloat16

D_MODEL = 1024
D_FF = 2816
S5_WIDTH = 256
S5_GROUP_DIM = 16
S5_GROUPS = 16
S5_STATE = 64
S5_CH = S5_GROUPS * S5_STATE
RET_WIDTH = 512
RET_HEAD_DIM = 128
RET_HEADS = 4
RET_CHUNK = 128
LRU_WIDTH = 256
LRU_BLOCKS = 4
LRU_BLOCK_DIM = 64
CONV_WIDTH = 4
LRU_C = 8.0
ROPE_BASE = 10000.0
NORM_EPS = 1e-6
IN_SECTIONS = (S5_WIDTH, RET_WIDTH, RET_WIDTH, RET_WIDTH, RET_WIDTH, LRU_WIDTH, LRU_WIDTH)
IN_WIDTH = sum(IN_SECTIONS)

SUBLANES = 8
LANES = 128
MIB = 1024 * 1024

FFN_ROWS = 512
FFN_CHUNK = 256
PROJ_ROWS = 512
S5_ROWS = 512
LRU_ROWS = 1024
LRU_UNROLL = 4
RET_ROWS = 512


def _params(vmem_mib, n_axes):
    return pltpu.CompilerParams(
        dimension_semantics=("arbitrary",) * n_axes,
        vmem_limit_bytes=vmem_mib * MIB)


def _rms_rows(x):
    return x * lax.rsqrt(jnp.mean(x * x, axis=-1, keepdims=True) + NORM_EPS)


def _gelu_tanh(x):
    c = math.sqrt(2.0 / math.pi)
    return x * (0.5 * (1.0 + jnp.tanh(c * (x + 0.044715 * (x * x * x)))))


def _sigmoid(x):
    return 0.5 + 0.5 * jnp.tanh(0.5 * x)


def _silu(x):
    h = 0.5 * x
    return h + h * jnp.tanh(h)


def _dot(a, b):
    return jnp.dot(a, b, preferred_element_type=F32)


def _resident(shape, layer):
    nd = len(shape)
    return pl.BlockSpec((None,) + tuple(shape), lambda *_: (layer,) + (0,) * nd,
                        pipeline_mode=pl.Buffered(1))


def _ffn_kernel(*refs, with_outproj, with_final_norm):
    refs = list(refs)
    o_ref = refs.pop()
    x = refs.pop(0)[...]
    if with_outproj:
        ys_ref, yr_ref, yl_ref, wo_ref = refs[:4]
        refs = refs[4:]
        x = x + _dot(ys_ref[...], wo_ref[0:S5_WIDTH, :])
        x = x + _dot(yr_ref[...], wo_ref[S5_WIDTH:S5_WIDTH + RET_WIDTH, :])
        x = x + _dot(yl_ref[...], wo_ref[S5_WIDTH + RET_WIDTH:, :])
    g_ref, wg_ref, wu_ref, wd_ref = refs[:4]
    h = (_rms_rows(x) * g_ref[...]).astype(BF16)
    acc = jnp.zeros(x.shape, F32)
    for c in range(D_FF // FFN_CHUNK):
        sl = slice(c * FFN_CHUNK, (c + 1) * FFN_CHUNK)
        act = (_silu(_dot(h, wg_ref[:, sl])) * _dot(h, wu_ref[:, sl])).astype(BF16)
        acc = acc + _dot(act, wd_ref[sl, :])
    y = x + 0.5 * acc
    if with_final_norm:
        y = _rms_rows(y) * refs[4][...]
    o_ref[...] = y


def _ffn(x, gain, wg, wu, wd, layer, mix=None, final_gain=None):
    t = x.shape[0]
    spec = lambda w: pl.BlockSpec((FFN_ROWS, w), lambda i: (i, 0))
    operands = [x]
    in_specs = [spec(D_MODEL)]
    if mix is not None:
        y_s5, y_ret, y_lru, w_out = mix
        operands += [y_s5, y_ret, y_lru, w_out]
        in_specs += [spec(S5_WIDTH), spec(RET_WIDTH), spec(LRU_WIDTH),
                     _resident((D_MODEL, D_MODEL), layer)]
    operands += [gain, wg, wu, wd]
    in_specs += [_resident((1, D_MODEL), layer),
                 _resident((D_MODEL, D_FF), layer),
                 _resident((D_MODEL, D_FF), layer),
                 _resident((D_FF, D_MODEL), layer)]
    if final_gain is not None:
        operands.append(final_gain)
        in_specs.append(pl.BlockSpec((1, D_MODEL), lambda i: (0, 0)))
    return pl.pallas_call(
        functools.partial(_ffn_kernel, with_outproj=mix is not None,
                          with_final_norm=final_gain is not None),
        grid=(t // FFN_ROWS,),
        in_specs=in_specs,
        out_specs=spec(D_MODEL),
        out_shape=jax.ShapeDtypeStruct(x.shape, F32),
        compiler_params=_params(50, 1),
        name="ffn_mix" if mix is not None else "ffn",
    )(*operands)


K_SECTION = 2


def _inproj_kernel(x_ref, g_ref, w_ref, wkt_ref, *out_refs):
    h = (_rms_rows(x_ref[...]) * g_ref[...]).astype(BF16)
    off = 0
    for n, (o_ref, width) in enumerate(zip(out_refs, IN_SECTIONS)):
        if n == K_SECTION:
            o_ref[...] = lax.dot_general(wkt_ref[...], h, (((1,), (1,)), ((), ())),
                                         preferred_element_type=F32)
        else:
            o_ref[...] = _dot(h, w_ref[:, off:off + width])
        off += width


def _inproj(x, gain, w_in, w_k_t, layer):
    t = x.shape[0]
    row = lambda w: pl.BlockSpec((PROJ_ROWS, w), lambda i: (i, 0))
    out_specs = [row(w) for w in IN_SECTIONS]
    out_shape = [jax.ShapeDtypeStruct((t, w), F32) for w in IN_SECTIONS]
    out_specs[K_SECTION] = pl.BlockSpec((RET_WIDTH, PROJ_ROWS), lambda i: (0, i))
    out_shape[K_SECTION] = jax.ShapeDtypeStruct((RET_WIDTH, t), F32)
    return pl.pallas_call(
        _inproj_kernel,
        grid=(t // PROJ_ROWS,),
        in_specs=[row(D_MODEL),
                  _resident((1, D_MODEL), layer),
                  _resident((D_MODEL, IN_WIDTH), layer),
                  _resident((RET_WIDTH, D_MODEL), layer)],
        out_specs=out_specs,
        out_shape=out_shape,
        compiler_params=_params(40, 1),
        name="inproj",
    )(x, gain, w_in, w_k_t)


S5_CONST_ROWS = 8 * SUBLANES


def _cmul(a, b):
    return a[0] * b[0] - a[1] * b[1], a[0] * b[1] + a[1] * b[0]


def _s5_disc_kernel(lr_ref, li_ref, ls_ref, er_ref, ei_ref, b_ref, c_ref):
    lr = lr_ref[...]
    li = li_ref[...]
    step = jnp.exp(ls_ref[...])
    ar = lr * step
    ai = li * step
    mag = jnp.exp(ar)
    p1 = (mag * jnp.cos(ai), mag * jnp.sin(ai))
    nr = p1[0] - 1.0
    den = lr * lr + li * li
    fr = (nr * lr + p1[1] * li) / den
    fi = (p1[1] * lr - nr * li) / den
    er = er_ref[...]
    ei = ei_ref[...]
    b_ref[:, :S5_CH] = (fr * er - fi * ei).astype(BF16)
    b_ref[:, S5_CH:] = (fr * ei + fi * er).astype(BF16)
    p2 = _cmul(p1, p1)
    p3 = _cmul(p2, p1)
    p4 = _cmul(p2, p2)
    powers = [p1, p2, p3, p4, _cmul(p4, p1), _cmul(p4, p2), _cmul(p4, p3), _cmul(p4, p4)]
    row = lax.broadcasted_iota(jnp.int32, (SUBLANES, S5_CH), 0)
    zero = jnp.zeros((SUBLANES, S5_CH), F32)
    for s, (d, p) in enumerate(((1, p1), (2, p2), (4, p4))):
        for part in range(2):
            base = (2 * s + part) * SUBLANES
            c_ref[base:base + SUBLANES, :] = jnp.where(
                row >= d, jnp.broadcast_to(p[part], (SUBLANES, S5_CH)), zero)
    for part in range(2):
        tile = zero
        for r in range(SUBLANES):
            tile = jnp.where(row == r, jnp.broadcast_to(powers[r][part], (SUBLANES, S5_CH)), tile)
        base = (6 + part) * SUBLANES
        c_ref[base:base + SUBLANES, :] = tile


def _s5_discretise(lam_re, lam_im, log_step, b_re, b_im):
    depth = lam_re.shape[0]
    eye = jnp.eye(S5_GROUPS, dtype=F32)

    def expand(b):
        return jnp.einsum('lgph,gk->lghkp', b, eye).reshape(depth, S5_WIDTH, S5_CH)

    vec = lambda a: a.reshape(depth, 1, S5_CH)
    ls = jnp.broadcast_to(log_step[:, :, None], (depth, S5_GROUPS, S5_STATE))
    vspec = pl.BlockSpec((None, 1, S5_CH), lambda l: (l, 0, 0))
    mspec = pl.BlockSpec((None, S5_WIDTH, S5_CH), lambda l: (l, 0, 0))
    return pl.pallas_call(
        _s5_disc_kernel,
        grid=(depth,),
        in_specs=[vspec, vspec, vspec, mspec, mspec],
        out_specs=[pl.BlockSpec((None, S5_WIDTH, 2 * S5_CH), lambda l: (l, 0, 0)),
                   pl.BlockSpec((None, S5_CONST_ROWS, S5_CH), lambda l: (l, 0, 0))],
        out_shape=[jax.ShapeDtypeStruct((depth, S5_WIDTH, 2 * S5_CH), BF16),
                   jax.ShapeDtypeStruct((depth, S5_CONST_ROWS, S5_CH), F32)],
        compiler_params=_params(32, 1),
        name="s5_disc",
    )(vec(lam_re), vec(lam_im), vec(ls), expand(b_re), expand(b_im))


def _s5_kernel(u_ref, b_ref, c_ref, k_ref, d_ref, wglu_ref, bglu_ref, gain_ref, o_ref,
               bu_scr, s_scr, carry_scr):
    j = pl.program_id(1)
    rows = u_ref.shape[0]
    ncol = S5_CH // LANES
    cols = [(slice(cb * LANES, (cb + 1) * LANES),
             slice(S5_CH + cb * LANES, S5_CH + (cb + 1) * LANES)) for cb in range(ncol)]

    @pl.when(j == 0)
    def _():
        carry_scr[...] = jnp.zeros(carry_scr.shape, F32)

    u = u_ref[...]
    bu_scr[...] = _dot(u.astype(BF16), b_ref[...])

    def tile_body(i, carry):
        off = pl.multiple_of(i * SUBLANES, SUBLANES)
        loaded = [(bu_scr[pl.ds(off, SUBLANES), re], bu_scr[pl.ds(off, SUBLANES), im])
                  for re, im in cols]
        done = []
        for cb, (br, bi) in enumerate(loaded):
            re = cols[cb][0]
            for s, d in enumerate((1, 2, 4)):
                cr = k_ref[(2 * s) * SUBLANES:(2 * s + 1) * SUBLANES, re]
                ci = k_ref[(2 * s + 1) * SUBLANES:(2 * s + 2) * SUBLANES, re]
                rr = pltpu.roll(br, d, 0)
                ri = pltpu.roll(bi, d, 0)
                br, bi = br + (cr * rr - ci * ri), bi + (cr * ri + ci * rr)
            pr, pi = carry[cb], carry[ncol + cb]
            wr = k_ref[6 * SUBLANES:7 * SUBLANES, re]
            wi = k_ref[7 * SUBLANES:8 * SUBLANES, re]
            done.append((br + (wr * pr - wi * pi), bi + (wr * pi + wi * pr)))
        for (re, im), (sr, si) in zip(cols, done):
            s_scr[pl.ds(off, SUBLANES), re] = sr
            s_scr[pl.ds(off, SUBLANES), im] = si
        last = lambda a: jnp.broadcast_to(a[SUBLANES - 1:SUBLANES, :], (SUBLANES, LANES))
        return tuple(last(sr) for sr, _ in done) + tuple(last(si) for _, si in done)

    init = tuple(carry_scr[:, re] for re, _ in cols) + tuple(carry_scr[:, im] for _, im in cols)
    final = lax.fori_loop(0, rows // SUBLANES, tile_body, init)
    for cb, (re, im) in enumerate(cols):
        carry_scr[:, re] = final[cb]
        carry_scr[:, im] = final[ncol + cb]

    y = _dot(s_scr[...].astype(BF16), c_ref[...]) + d_ref[...] * u
    y = _gelu_tanh(y)
    y = y * _sigmoid(_dot(y.astype(BF16), wglu_ref[...]) + bglu_ref[...])
    o_ref[...] = (_rms_rows(y) * gain_ref[...]).astype(BF16)


def _s5(u, bblk, cblk, consts, d_skip, w_glu, b_glu, gain, layer, batch):
    t = u.shape[0]
    nblk = t // batch // S5_ROWS
    row = pl.BlockSpec((S5_ROWS, S5_WIDTH), lambda b, j: (b * nblk + j, 0))
    return pl.pallas_call(
        _s5_kernel,
        grid=(batch, nblk),
        in_specs=[row,
                  _resident((S5_WIDTH, 2 * S5_CH), layer),
                  _resident((2 * S5_CH, S5_WIDTH), layer),
                  _resident((S5_CONST_ROWS, S5_CH), layer),
                  _resident((1, S5_WIDTH), layer),
                  _resident((S5_WIDTH, S5_WIDTH), layer),
                  _resident((1, S5_WIDTH), layer),
                  _resident((1, S5_WIDTH), layer)],
        out_specs=row,
        out_shape=jax.ShapeDtypeStruct((t, S5_WIDTH), BF16),
        scratch_shapes=[pltpu.VMEM((S5_ROWS, 2 * S5_CH), F32),
                        pltpu.VMEM((S5_ROWS, 2 * S5_CH), F32),
                        pltpu.VMEM((SUBLANES, 2 * S5_CH), F32)],
        compiler_params=_params(40, 2),
        name="s5",
    )(u, bblk, cblk, consts, d_skip, w_glu, b_glu, gain)


def _lru_kernel(x_ref, gate_ref, cw_ref, cb_ref, wax_ref, bax_ref, lam_ref, gain_ref, o_ref,
                ext_scr, a_scr, b_scr, h_scr, carry_scr):
    j = pl.program_id(1)
    rows = x_ref.shape[0]
    pad = SUBLANES

    @pl.when(j == 0)
    def _():
        ext_scr[0:pad, :] = jnp.zeros((pad, LRU_WIDTH), F32)
        carry_scr[...] = jnp.zeros(carry_scr.shape, F32)

    @pl.when(j > 0)
    def _():
        ext_scr[0:pad, :] = ext_scr[rows:rows + pad, :]

    ext_scr[pad:rows + pad, :] = x_ref[...]
    xc = cb_ref[...] + cw_ref[CONV_WIDTH - 1:CONV_WIDTH, :] * ext_scr[pad:rows + pad, :]
    for lag in range(1, CONV_WIDTH):
        tap = cw_ref[CONV_WIDTH - 1 - lag:CONV_WIDTH - lag, :]
        xc = xc + tap * ext_scr[pad - lag:rows + pad - lag, :]

    gates = _sigmoid(_dot(xc.astype(BF16), wax_ref[...]) + bax_ref[...])
    r = gates[:, :LRU_WIDTH]
    i = gates[:, LRU_WIDTH:]
    z = -lam_ref[...]
    softplus = jnp.maximum(z, 0.0) + jnp.log1p(jnp.exp(-jnp.abs(z)))
    a = jnp.exp((-LRU_C) * r * softplus)
    a_scr[...] = a
    b_scr[...] = jnp.sqrt(1.0 - a * a) * (i * xc)

    row = lax.broadcasted_iota(jnp.int32, (SUBLANES, LRU_WIDTH), 0)

    def tile_body(t, prev):
        off = pl.multiple_of(t * SUBLANES, SUBLANES)
        at = a_scr[pl.ds(off, SUBLANES), :]
        bt = b_scr[pl.ds(off, SUBLANES), :]
        for d in (1, 2, 4):
            keep = row >= d
            ar = jnp.where(keep, pltpu.roll(at, d, 0), 1.0)
            br = jnp.where(keep, pltpu.roll(bt, d, 0), 0.0)
            bt = at * br + bt
            at = at * ar
        h = at * prev + bt
        h_scr[pl.ds(off, SUBLANES), :] = h
        return jnp.broadcast_to(h[SUBLANES - 1:SUBLANES, :], (SUBLANES, LRU_WIDTH))

    carry_scr[...] = lax.fori_loop(0, rows // SUBLANES, tile_body, carry_scr[...],
                                   unroll=LRU_UNROLL)

    y = h_scr[...] * _gelu_tanh(gate_ref[...])
    o_ref[...] = (_rms_rows(y) * gain_ref[...]).astype(BF16)


def _lru(x, gate, conv_w, conv_b, wax, bax, lam, gain, layer, batch):
    t = x.shape[0]
    nblk = t // batch // LRU_ROWS
    row = pl.BlockSpec((LRU_ROWS, LRU_WIDTH), lambda b, j: (b * nblk + j, 0))
    return pl.pallas_call(
        _lru_kernel,
        grid=(batch, nblk),
        in_specs=[row, row,
                  _resident((CONV_WIDTH, LRU_WIDTH), layer),
                  _resident((1, LRU_WIDTH), layer),
                  _resident((LRU_WIDTH, 2 * LRU_WIDTH), layer),
                  _resident((1, 2 * LRU_WIDTH), layer),
                  _resident((1, LRU_WIDTH), layer),
                  _resident((1, LRU_WIDTH), layer)],
        out_specs=row,
        out_shape=jax.ShapeDtypeStruct((t, LRU_WIDTH), BF16),
        scratch_shapes=[pltpu.VMEM((LRU_ROWS + SUBLANES, LRU_WIDTH), F32),
                        pltpu.VMEM((LRU_ROWS, LRU_WIDTH), F32),
                        pltpu.VMEM((LRU_ROWS, LRU_WIDTH), F32),
                        pltpu.VMEM((LRU_ROWS, LRU_WIDTH), F32),
                        pltpu.VMEM((SUBLANES, LRU_WIDTH), F32)],
        compiler_params=_params(32, 2),
        name="lru",
    )(x, gate, conv_w, conv_b, wax, bax, lam, gain)


def _ret_kernel(q_ref, kt_ref, v_ref, g_ref, cos_ref, sin_ref, cost_ref, sint_ref, decay_ref,
                zetat_ref, xi_ref, gamma_ref, gain_ref, o_ref, state_scr, y_scr):
    j = pl.program_id(1)

    @pl.when(j == 0)
    def _():
        state_scr[...] = jnp.zeros(state_scr.shape, F32)

    scale = RET_HEAD_DIM ** -0.5
    half = RET_HEAD_DIM // 2
    for c in range(q_ref.shape[0] // RET_CHUNK):
        rows = slice(c * RET_CHUNK, (c + 1) * RET_CHUNK)
        cs = cos_ref[rows, :]
        sn = sin_ref[rows, :]
        cst = cost_ref[:, rows]
        snt = sint_ref[:, rows]
        ssq = jnp.zeros((RET_CHUNK, 1), F32)
        for h in range(RET_HEADS):
            cols = slice(h * RET_HEAD_DIM, (h + 1) * RET_HEAD_DIM)
            qh = q_ref[rows, cols]
            kt = kt_ref[cols, rows]
            vh = v_ref[rows, cols].astype(BF16)
            qb = ((qh * cs + pltpu.roll(qh, half, 1) * sn) * scale).astype(BF16)
            kr = kt * cst + jnp.concatenate([kt[half:], kt[:half]], axis=0) * snt
            scores = _dot(qb, kr.astype(BF16)) * decay_ref[:, cols]
            inner = _dot(scores.astype(BF16), vh)
            st = state_scr[:, cols]
            cross = _dot(qb, st.astype(BF16)) * xi_ref[:, cols]
            kz = (kr * zetat_ref[cols, :]).astype(BF16)
            state_scr[:, cols] = gamma_ref[:, cols] * st + _dot(kz, vh)
            yh = _silu(g_ref[rows, cols]) * _rms_rows(inner + cross)
            ssq = ssq + jnp.sum(yh * yh, axis=-1, keepdims=True)
            y_scr[:, cols] = yh
        inv = lax.rsqrt(ssq * (1.0 / RET_WIDTH) + NORM_EPS)
        o_ref[rows, :] = ((y_scr[...] * inv) * gain_ref[...]).astype(BF16)


def _ret_tables(seq):
    dh, c, nh = RET_HEAD_DIM, RET_CHUNK, RET_HEADS
    pos = jnp.arange(seq, dtype=F32)
    inv_freq = ROPE_BASE ** (-jnp.arange(0, dh, 2, dtype=F32) / dh)
    ang = pos[:, None] * inv_freq[None, :]
    cos = jnp.cos(ang)
    sin = jnp.sin(ang)
    cos2 = jnp.concatenate([cos, cos], axis=-1)
    sin2 = jnp.concatenate([-sin, sin], axis=-1)
    log_gamma = jnp.log1p(-jnp.exp2(-5.0 - jnp.arange(nh, dtype=F32)))
    idx = jnp.arange(c, dtype=F32)
    diff = idx[:, None] - idx[None, :]
    decay = jnp.where(diff[None] >= 0,
                      jnp.exp(jnp.maximum(diff, 0.0)[None] * log_gamma[:, None, None]), 0.0)
    zeta = jnp.exp((c - 1.0 - idx)[None] * log_gamma[:, None])
    xi = jnp.exp((idx + 1.0)[None] * log_gamma[:, None])
    gamma_chunk = jnp.exp(c * log_gamma)
    decay_l = decay.transpose(1, 0, 2).reshape(c, nh * c)
    zeta_t = jnp.broadcast_to(zeta[:, None, :], (nh, dh, c)).reshape(nh * dh, c)
    xi_l = jnp.broadcast_to(xi.T[:, :, None], (c, nh, dh)).reshape(c, nh * dh)
    gamma_l = jnp.broadcast_to(gamma_chunk[:, None], (nh, dh)).reshape(1, nh * dh)
    return cos2, sin2, cos2.T, sin2.T, decay_l, zeta_t, xi_l, gamma_l


def _ret(q, k_t, v, g, tables, gain, layer, batch):
    t = q.shape[0]
    nblk = t // batch // RET_ROWS
    row = pl.BlockSpec((RET_ROWS, RET_WIDTH), lambda b, j: (b * nblk + j, 0))
    row_t = pl.BlockSpec((RET_WIDTH, RET_ROWS), lambda b, j: (0, b * nblk + j))
    rot = pl.BlockSpec((RET_ROWS, RET_HEAD_DIM), lambda b, j: (j, 0))
    rot_t = pl.BlockSpec((RET_HEAD_DIM, RET_ROWS), lambda b, j: (0, j))
    const = lambda shape: pl.BlockSpec(shape, lambda b, j: (0, 0), pipeline_mode=pl.Buffered(1))
    return pl.pallas_call(
        _ret_kernel,
        grid=(batch, nblk),
        in_specs=[row, row_t, row, row, rot, rot, rot_t, rot_t,
                  const((RET_CHUNK, RET_WIDTH)), const((RET_WIDTH, RET_CHUNK)),
                  const((RET_CHUNK, RET_WIDTH)), const((1, RET_WIDTH)),
                  _resident((1, RET_WIDTH), layer)],
        out_specs=row,
        out_shape=jax.ShapeDtypeStruct((t, RET_WIDTH), BF16),
        scratch_shapes=[pltpu.VMEM((RET_HEAD_DIM, RET_WIDTH), F32),
                        pltpu.VMEM((RET_CHUNK, RET_WIDTH), F32)],
        compiler_params=_params(32, 2),
        name="ret",
    )(q, k_t, v, g, *tables, gain)


def _block_diag(w):
    depth, n, d, e = w.shape
    eye = jnp.eye(n, dtype=w.dtype)
    return jnp.einsum('lnde,nm->lndme', w, eye).reshape(depth, n * d, n * e)


def kernel(x, ffn1_norm, ffn1_w_gate, ffn1_w_up, ffn1_w_down, mix_norm, w_in, s5_lambda_re, s5_lambda_im, s5_log_step, s5_b_re, s5_b_im, s5_c_re, s5_c_im, s5_d, s5_w_glu, s5_b_glu, s5_out_norm, ret_out_norm, lru_conv_w, lru_conv_b, lru_w_a, lru_b_a, lru_w_x, lru_b_x, lru_lambda, lru_out_norm, w_out, ffn2_norm, ffn2_w_gate, ffn2_w_up, ffn2_w_down, final_norm):
    batch, seq, d = x.shape
    depth = w_in.shape[0]
    t = batch * seq
    assert d == D_MODEL and seq % max(S5_ROWS, LRU_ROWS, RET_ROWS) == 0
    assert t % max(FFN_ROWS, PROJ_ROWS) == 0

    row3 = lambda a: a.reshape(depth, 1, a.shape[-1])
    bf = lambda a: a.astype(BF16)

    f1 = (row3(ffn1_norm), bf(ffn1_w_gate), bf(ffn1_w_up), bf(ffn1_w_down))
    f2 = (row3(ffn2_norm), bf(ffn2_w_gate), bf(ffn2_w_up), bf(ffn2_w_down))
    w_in_b = bf(w_in)
    k_off = sum(IN_SECTIONS[:K_SECTION])
    w_k_t = bf(jnp.swapaxes(w_in[:, :, k_off:k_off + RET_WIDTH], 1, 2))
    w_out_b = bf(w_out)
    eye_g = jnp.eye(S5_GROUPS, dtype=F32)
    c_blk = lambda c: jnp.einsum('lghp,gk->lgpkh', c, eye_g).reshape(depth, S5_CH, S5_WIDTH)
    s5_cblk = bf(jnp.concatenate([c_blk(s5_c_re), -c_blk(s5_c_im)], axis=1))
    s5_bblk, s5_consts = _s5_discretise(s5_lambda_re, s5_lambda_im, s5_log_step, s5_b_re, s5_b_im)
    lru_wax = bf(jnp.concatenate([_block_diag(lru_w_a), _block_diag(lru_w_x)], axis=-1))
    lru_bax = row3(jnp.concatenate([lru_b_a, lru_b_x], axis=-1))
    tables = _ret_tables(seq)

    xt = x.reshape(t, d)
    for l in range(depth):
        xt = _ffn(xt, *f1, l)
        u, q, k_t, v, g, xl, gl = _inproj(xt, row3(mix_norm), w_in_b, w_k_t, l)
        y_s5 = _s5(u, s5_bblk, s5_cblk, s5_consts, row3(s5_d), bf(s5_w_glu), row3(s5_b_glu),
                   row3(s5_out_norm), l, batch)
        y_ret = _ret(q, k_t, v, g, tables, row3(ret_out_norm), l, batch)
        y_lru = _lru(xl, gl, lru_conv_w, row3(lru_conv_b), lru_wax, lru_bax, row3(lru_lambda),
                     row3(lru_out_norm), l, batch)
        last = final_norm.reshape(1, d) if l == depth - 1 else None
        xt = _ffn(xt, *f2, l, mix=(y_s5, y_ret, y_lru, w_out_b), final_gain=last)
    return xt.reshape(batch, seq, d)
```

```python
import functools
import math

import jax
import jax.numpy as jnp
from jax import lax
from jax.experimental import pallas as pl
from jax.experimental.pallas import tpu as pltpu

F32 = jnp.float32
BF16 = jnp.bfloat16

D_MODEL = 1024
D_FF = 2816
S5_WIDTH = 256
S5_GROUP_DIM = 16
S5_GROUPS = 16
S5_STATE = 64
S5_CH = S5_GROUPS * S5_STATE
RET_WIDTH = 512
RET_HEAD_DIM = 128
RET_HEADS = 4
RET_CHUNK = 128
LRU_WIDTH = 256
LRU_BLOCKS = 4
LRU_BLOCK_DIM = 64
CONV_WIDTH = 4
LRU_C = 8.0
ROPE_BASE = 10000.0
NORM_EPS = 1e-6
IN_SECTIONS = (S5_WIDTH, RET_WIDTH, RET_WIDTH, RET_WIDTH, RET_WIDTH, LRU_WIDTH, LRU_WIDTH)
IN_WIDTH = sum(IN_SECTIONS)

SUBLANES = 8
LANES = 128
MIB = 1024 * 1024

FFN_ROWS = 512
FFN_CHUNK = 256
PROJ_ROWS = 512
S5_STATE_ROWS = 4096
S5_OUT_ROWS = 2048
LRU_ROWS = 1024
LRU_UNROLL = 4
RET_ROWS = 512


def _params(vmem_mib, n_axes):
    return pltpu.CompilerParams(
        dimension_semantics=("arbitrary",) * n_axes,
        vmem_limit_bytes=vmem_mib * MIB)


def _rms_rows(x):
    return x * lax.rsqrt(jnp.mean(x * x, axis=-1, keepdims=True) + NORM_EPS)


def _gelu_tanh(x):
    c = math.sqrt(2.0 / math.pi)
    return x * (0.5 * (1.0 + jnp.tanh(c * (x + 0.044715 * (x * x * x)))))


def _sigmoid(x):
    return 0.5 + 0.5 * jnp.tanh(0.5 * x)


def _silu(x):
    h = 0.5 * x
    return h + h * jnp.tanh(h)


def _dot(a, b):
    return jnp.dot(a, b, preferred_element_type=F32)


def _resident(shape, layer):
    nd = len(shape)
    return pl.BlockSpec((None,) + tuple(shape), lambda *_: (layer,) + (0,) * nd,
                        pipeline_mode=pl.Buffered(1))


def _ffn_kernel(*refs, with_outproj, with_final_norm):
    refs = list(refs)
    o_ref = refs.pop()
    x = refs.pop(0)[...]
    if with_outproj:
        ys_ref, yr_ref, yl_ref, wo_ref = refs[:4]
        refs = refs[4:]
        x = x + _dot(ys_ref[...], wo_ref[0:S5_WIDTH, :])
        x = x + _dot(yr_ref[...], wo_ref[S5_WIDTH:S5_WIDTH + RET_WIDTH, :])
        x = x + _dot(yl_ref[...], wo_ref[S5_WIDTH + RET_WIDTH:, :])
    g_ref, wg_ref, wu_ref, wd_ref = refs[:4]
    h = (_rms_rows(x) * g_ref[...]).astype(BF16)
    acc = jnp.zeros(x.shape, F32)
    for c in range(D_FF // FFN_CHUNK):
        sl = slice(c * FFN_CHUNK, (c + 1) * FFN_CHUNK)
        act = (_silu(_dot(h, wg_ref[:, sl])) * _dot(h, wu_ref[:, sl])).astype(BF16)
        acc = acc + _dot(act, wd_ref[sl, :])
    y = x + 0.5 * acc
    if with_final_norm:
        y = _rms_rows(y) * refs[4][...]
    o_ref[...] = y


def _ffn(x, gain, wg, wu, wd, layer, mix=None, final_gain=None):
    t = x.shape[0]
    spec = lambda w: pl.BlockSpec((FFN_ROWS, w), lambda i: (i, 0))
    operands = [x]
    in_specs = [spec(D_MODEL)]
    if mix is not None:
        y_s5, y_ret, y_lru, w_out = mix
        operands += [y_s5, y_ret, y_lru, w_out]
        in_specs += [spec(S5_WIDTH), spec(RET_WIDTH), spec(LRU_WIDTH),
                     _resident((D_MODEL, D_MODEL), layer)]
    operands += [gain, wg, wu, wd]
    in_specs += [_resident((1, D_MODEL), layer),
                 _resident((D_MODEL, D_FF), layer),
                 _resident((D_MODEL, D_FF), layer),
                 _resident((D_FF, D_MODEL), layer)]
    if final_gain is not None:
        operands.append(final_gain)
        in_specs.append(pl.BlockSpec((1, D_MODEL), lambda i: (0, 0)))
    return pl.pallas_call(
        functools.partial(_ffn_kernel, with_outproj=mix is not None,
                          with_final_norm=final_gain is not None),
        grid=(t // FFN_ROWS,),
        in_specs=in_specs,
        out_specs=spec(D_MODEL),
        out_shape=jax.ShapeDtypeStruct(x.shape, F32),
        compiler_params=_params(50, 1),
        name="ffn_mix" if mix is not None else "ffn",
    )(*operands)


K_SECTION = 2


def _inproj_kernel(x_ref, g_ref, w_ref, wkt_ref, *out_refs):
    h = (_rms_rows(x_ref[...]) * g_ref[...]).astype(BF16)
    off = 0
    for n, (o_ref, width) in enumerate(zip(out_refs, IN_SECTIONS)):
        if n == K_SECTION:
            o_ref[...] = lax.dot_general(wkt_ref[...], h, (((1,), (1,)), ((), ())),
                                         preferred_element_type=F32)
        else:
            o_ref[...] = _dot(h, w_ref[:, off:off + width])
        off += width


def _inproj(x, gain, w_in, w_k_t, layer):
    t = x.shape[0]
    row = lambda w: pl.BlockSpec((PROJ_ROWS, w), lambda i: (i, 0))
    out_specs = [row(w) for w in IN_SECTIONS]
    out_shape = [jax.ShapeDtypeStruct((t, w), F32) for w in IN_SECTIONS]
    out_specs[K_SECTION] = pl.BlockSpec((RET_WIDTH, PROJ_ROWS), lambda i: (0, i))
    out_shape[K_SECTION] = jax.ShapeDtypeStruct((RET_WIDTH, t), F32)
    return pl.pallas_call(
        _inproj_kernel,
        grid=(t // PROJ_ROWS,),
        in_specs=[row(D_MODEL),
                  _resident((1, D_MODEL), layer),
                  _resident((D_MODEL, IN_WIDTH), layer),
                  _resident((RET_WIDTH, D_MODEL), layer)],
        out_specs=out_specs,
        out_shape=out_shape,
        compiler_params=_params(40, 1),
        name="inproj",
    )(x, gain, w_in, w_k_t)


S5_CHUNK = 8
S5_CAT = S5_CHUNK * S5_WIDTH
S5_CONST_ROWS = 8 * SUBLANES


def _cmul(a, b):
    return a[0] * b[0] - a[1] * b[1], a[0] * b[1] + a[1] * b[0]


def _s5_prep_kernel(lr_ref, li_ref, ls_ref, er_ref, ei_ref, ctr_ref, cti_ref, cst_ref,
                    wst_ref, kk_ref, vt_ref, c_ref):
    k = pl.program_id(1)
    lr = lr_ref[...]
    li = li_ref[...]
    step = jnp.exp(ls_ref[...])
    ar = lr * step
    ai = li * step
    mag = jnp.exp(ar)
    p1 = (mag * jnp.cos(ai), mag * jnp.sin(ai))
    nr = p1[0] - 1.0
    den = lr * lr + li * li
    f = ((nr * lr + p1[1] * li) / den, (p1[1] * lr - nr * li) / den)
    powers = [(jnp.ones_like(lr), jnp.zeros_like(lr)), p1]
    for _ in range(2, S5_CHUNK + 1):
        powers.append(_cmul(powers[-1], p1))

    def pick(n):
        out = powers[0]
        for i in range(1, S5_CHUNK + 1):
            out = (jnp.where(n == i, powers[i][0], out[0]), jnp.where(n == i, powers[i][1], out[1]))
        return out

    fp = _cmul(f, pick(k))
    er = er_ref[...]
    ei = ei_ref[...]
    w = jnp.concatenate([fp[0] * er - fp[1] * ei, fp[0] * ei + fp[1] * er], axis=1)
    wb = w.astype(BF16)
    wst_ref[...] = wb
    kk_ref[...] = _dot(wb, cst_ref[...])
    a, b = pick(k + 1)
    ctr = ctr_ref[...]
    cti = cti_ref[...]
    vt_ref[:, :S5_CH] = (ctr * a - cti * b).astype(BF16)
    vt_ref[:, S5_CH:] = (-(ctr * b) - cti * a).astype(BF16)

    @pl.when(k == 0)
    def _():
        q = [powers[S5_CHUNK]]
        for _ in range(1, SUBLANES):
            q.append(_cmul(q[-1], q[0]))
        row = lax.broadcasted_iota(jnp.int32, (SUBLANES, S5_CH), 0)
        zero = jnp.zeros((SUBLANES, S5_CH), F32)
        for s, d in enumerate((1, 2, 4)):
            for part in range(2):
                base = (2 * s + part) * SUBLANES
                c_ref[base:base + SUBLANES, :] = jnp.where(
                    row >= d, jnp.broadcast_to(q[d - 1][part], (SUBLANES, S5_CH)), zero)
        for part in range(2):
            tile = zero
            for r in range(SUBLANES):
                tile = jnp.where(row == r, jnp.broadcast_to(q[r][part], (SUBLANES, S5_CH)), tile)
            base = (6 + part) * SUBLANES
            c_ref[base:base + SUBLANES, :] = tile


def _s5_prepare(lam_re, lam_im, log_step, b_re, b_im, c_re, c_im):
    depth = lam_re.shape[0]
    eye = jnp.eye(S5_GROUPS, dtype=F32)
    b_rows = lambda b: jnp.einsum('lgph,gk->lghkp', b, eye).reshape(depth, S5_WIDTH, S5_CH)
    c_rows = lambda c: jnp.einsum('lghp,gk->lghkp', c, eye).reshape(depth, S5_WIDTH, S5_CH)
    c_cols = lambda c: jnp.einsum('lghp,gk->lgpkh', c, eye).reshape(depth, S5_CH, S5_WIDTH)
    c_stack = jnp.concatenate([c_cols(c_re), -c_cols(c_im)], axis=1).astype(BF16)
    vec = lambda a: a.reshape(depth, 1, S5_CH)
    ls = jnp.broadcast_to(log_step[:, :, None], (depth, S5_GROUPS, S5_STATE))
    vspec = pl.BlockSpec((None, 1, S5_CH), lambda l, k: (l, 0, 0))
    mspec = pl.BlockSpec((None, S5_WIDTH, S5_CH), lambda l, k: (l, 0, 0))
    wst, kk, vt, consts = pl.pallas_call(
        _s5_prep_kernel,
        grid=(depth, S5_CHUNK),
        in_specs=[vspec, vspec, vspec, mspec, mspec, mspec, mspec,
                  pl.BlockSpec((None, 2 * S5_CH, S5_WIDTH), lambda l, k: (l, 0, 0))],
        out_specs=[pl.BlockSpec((None, S5_WIDTH, 2 * S5_CH),
                                lambda l, k: (l, S5_CHUNK - 1 - k, 0)),
                   pl.BlockSpec((None, None, S5_WIDTH, S5_WIDTH), lambda l, k: (l, k, 0, 0)),
                   pl.BlockSpec((None, S5_WIDTH, 2 * S5_CH), lambda l, k: (l, k, 0)),
                   pl.BlockSpec((None, S5_CONST_ROWS, S5_CH), lambda l, k: (l, 0, 0))],
        out_shape=[jax.ShapeDtypeStruct((depth, S5_CAT, 2 * S5_CH), BF16),
                   jax.ShapeDtypeStruct((depth, S5_CHUNK, S5_WIDTH, S5_WIDTH), F32),
                   jax.ShapeDtypeStruct((depth, S5_CAT, 2 * S5_CH), BF16),
                   jax.ShapeDtypeStruct((depth, S5_CONST_ROWS, S5_CH), F32)],
        compiler_params=_params(40, 2),
        name="s5_prep",
    )(vec(lam_re), vec(lam_im), vec(ls), b_rows(b_re), b_rows(b_im), c_rows(c_re), c_rows(c_im),
      c_stack)
    zero = jnp.zeros((depth, S5_WIDTH, S5_WIDTH), F32)
    m = jnp.concatenate(
        [jnp.concatenate([zero] * j + [kk[:, k] for k in range(S5_CHUNK - j)], axis=-1)
         for j in range(S5_CHUNK)], axis=-2).astype(BF16)
    return wst, m, vt, consts


def _s5_state_kernel(ulo_ref, uhi_ref, wst_ref, k_ref, uc_ref, sp_ref, ds_scr, s_scr):
    j = pl.program_id(1)
    nch = ulo_ref.shape[0] // S5_CHUNK
    ncol = S5_CH // LANES
    cols = [(slice(cb * LANES, (cb + 1) * LANES),
             slice(S5_CH + cb * LANES, S5_CH + (cb + 1) * LANES)) for cb in range(ncol)]

    @pl.when(j == 0)
    def _():
        s_scr[0:SUBLANES, :] = jnp.zeros((SUBLANES, 2 * S5_CH), F32)

    @pl.when(j > 0)
    def _():
        s_scr[0:SUBLANES, :] = s_scr[nch:nch + SUBLANES, :]

    uc = jnp.concatenate([half[pl.ds(jj, nch, stride=S5_CHUNK), :]
                          for jj in range(S5_CHUNK) for half in (ulo_ref, uhi_ref)],
                         axis=1).astype(BF16)
    uc_ref[...] = uc
    ds_scr[...] = _dot(uc, wst_ref[...])

    def tile_body(i, carry):
        off = pl.multiple_of(i * SUBLANES, SUBLANES)
        loaded = [(ds_scr[pl.ds(off, SUBLANES), re], ds_scr[pl.ds(off, SUBLANES), im])
                  for re, im in cols]
        done = []
        for cb, (br, bi) in enumerate(loaded):
            re = cols[cb][0]
            for s, d in enumerate((1, 2, 4)):
                cr = k_ref[(2 * s) * SUBLANES:(2 * s + 1) * SUBLANES, re]
                ci = k_ref[(2 * s + 1) * SUBLANES:(2 * s + 2) * SUBLANES, re]
                rr = pltpu.roll(br, d, 0)
                ri = pltpu.roll(bi, d, 0)
                br, bi = br + (cr * rr - ci * ri), bi + (cr * ri + ci * rr)
            pr, pi = carry[cb], carry[ncol + cb]
            wr = k_ref[6 * SUBLANES:7 * SUBLANES, re]
            wi = k_ref[7 * SUBLANES:8 * SUBLANES, re]
            done.append((br + (wr * pr - wi * pi), bi + (wr * pi + wi * pr)))
        for (re, im), (sr, si) in zip(cols, done):
            s_scr[pl.ds(off + SUBLANES, SUBLANES), re] = sr
            s_scr[pl.ds(off + SUBLANES, SUBLANES), im] = si
        return tuple(last(sr) for sr, _ in done) + tuple(last(si) for _, si in done)

    last = lambda a: jnp.broadcast_to(a[SUBLANES - 1:SUBLANES, :], (SUBLANES, LANES))
    init = (tuple(last(s_scr[0:SUBLANES, re]) for re, _ in cols)
            + tuple(last(s_scr[0:SUBLANES, im]) for _, im in cols))
    lax.fori_loop(0, nch // SUBLANES, tile_body, init)

    sp_ref[...] = s_scr[SUBLANES - 1:SUBLANES - 1 + nch, :].astype(BF16)


def _s5_state(u, wst, consts, layer, batch):
    t = u.shape[0]
    nblk = t // batch // S5_STATE_ROWS
    nch = S5_STATE_ROWS // S5_CHUNK
    chunk_rows = pl.BlockSpec((nch, S5_CAT), lambda b, j: (b * nblk + j, 0))
    return pl.pallas_call(
        _s5_state_kernel,
        grid=(batch, nblk),
        in_specs=[pl.BlockSpec((S5_STATE_ROWS, LANES), lambda b, j: (b * nblk + j, 0)),
                  pl.BlockSpec((S5_STATE_ROWS, LANES), lambda b, j: (b * nblk + j, 1)),
                  _resident((S5_CAT, 2 * S5_CH), layer),
                  _resident((S5_CONST_ROWS, S5_CH), layer)],
        out_specs=[chunk_rows, chunk_rows],
        out_shape=[jax.ShapeDtypeStruct((t // S5_CHUNK, S5_CAT), BF16),
                   jax.ShapeDtypeStruct((t // S5_CHUNK, 2 * S5_CH), BF16)],
        scratch_shapes=[pltpu.VMEM((nch, 2 * S5_CH), F32),
                        pltpu.VMEM((nch + SUBLANES, 2 * S5_CH), F32)],
        compiler_params=_params(44, 2),
        name="s5_state",
    )(u, u, wst, consts)


def _s5_out_kernel(u_ref, uc_ref, sp_ref, m_ref, vt_ref, d_ref, wglu_ref, bglu_ref, gain_ref,
                   o_ref, ylo_scr, yhi_scr):
    nch = uc_ref.shape[0]
    sp = sp_ref[...]
    for r in range(S5_CHUNK):
        cols = slice(r * S5_WIDTH, (r + 1) * S5_WIDTH)
        used = (r + 1) * S5_WIDTH
        y_r = _dot(uc_ref[:, :used], m_ref[:used, cols])
        y_r = y_r + lax.dot_general(sp, vt_ref[cols, :], (((1,), (1,)), ((), ())),
                                    preferred_element_type=F32)
        ylo_scr[pl.ds(r, nch, stride=S5_CHUNK), :] = y_r[:, :LANES]
        yhi_scr[pl.ds(r, nch, stride=S5_CHUNK), :] = y_r[:, LANES:]
    y = jnp.concatenate([ylo_scr[...], yhi_scr[...]], axis=1) + d_ref[...] * u_ref[...]
    y = _gelu_tanh(y)
    y = y * _sigmoid(_dot(y.astype(BF16), wglu_ref[...]) + bglu_ref[...])
    o_ref[...] = (_rms_rows(y) * gain_ref[...]).astype(BF16)


def _s5_out(u, uc, sp, m, vt, d_skip, w_glu, b_glu, gain, layer):
    t = u.shape[0]
    nch = S5_OUT_ROWS // S5_CHUNK
    row = pl.BlockSpec((S5_OUT_ROWS, S5_WIDTH), lambda i: (i, 0))
    chunk_rows = pl.BlockSpec((nch, S5_CAT), lambda i: (i, 0))
    return pl.pallas_call(
        _s5_out_kernel,
        grid=(t // S5_OUT_ROWS,),
        in_specs=[row, chunk_rows, chunk_rows,
                  _resident((S5_CAT, S5_CAT), layer),
                  _resident((S5_CAT, 2 * S5_CH), layer),
                  _resident((1, S5_WIDTH), layer),
                  _resident((S5_WIDTH, S5_WIDTH), layer),
                  _resident((1, S5_WIDTH), layer),
                  _resident((1, S5_WIDTH), layer)],
        out_specs=row,
        out_shape=jax.ShapeDtypeStruct((t, S5_WIDTH), BF16),
        scratch_shapes=[pltpu.VMEM((S5_OUT_ROWS, LANES), F32),
                        pltpu.VMEM((S5_OUT_ROWS, LANES), F32)],
        compiler_params=_params(48, 1),
        name="s5_out",
    )(u, uc, sp, m, vt, d_skip, w_glu, b_glu, gain)


def _lru_kernel(x_ref, gate_ref, cw_ref, cb_ref, wax_ref, bax_ref, lam_ref, gain_ref, o_ref,
                ext_scr, a_scr, b_scr, h_scr, carry_scr):
    j = pl.program_id(1)
    rows = x_ref.shape[0]
    pad = SUBLANES

    @pl.when(j == 0)
    def _():
        ext_scr[0:pad, :] = jnp.zeros((pad, LRU_WIDTH), F32)
        carry_scr[...] = jnp.zeros(carry_scr.shape, F32)

    @pl.when(j > 0)
    def _():
        ext_scr[0:pad, :] = ext_scr[rows:rows + pad, :]

    ext_scr[pad:rows + pad, :] = x_ref[...]
    xc = cb_ref[...] + cw_ref[CONV_WIDTH - 1:CONV_WIDTH, :] * ext_scr[pad:rows + pad, :]
    for lag in range(1, CONV_WIDTH):
        tap = cw_ref[CONV_WIDTH - 1 - lag:CONV_WIDTH - lag, :]
        xc = xc + tap * ext_scr[pad - lag:rows + pad - lag, :]

    gates = _sigmoid(_dot(xc.astype(BF16), wax_ref[...]) + bax_ref[...])
    r = gates[:, :LRU_WIDTH]
    i = gates[:, LRU_WIDTH:]
    z = -lam_ref[...]
    softplus = jnp.maximum(z, 0.0) + jnp.log1p(jnp.exp(-jnp.abs(z)))
    a = jnp.exp((-LRU_C) * r * softplus)
    a_scr[...] = a
    b_scr[...] = jnp.sqrt(1.0 - a * a) * (i * xc)

    row = lax.broadcasted_iota(jnp.int32, (SUBLANES, LRU_WIDTH), 0)

    def tile_body(t, prev):
        off = pl.multiple_of(t * SUBLANES, SUBLANES)
        at = a_scr[pl.ds(off, SUBLANES), :]
        bt = b_scr[pl.ds(off, SUBLANES), :]
        for d in (1, 2, 4):
            keep = row >= d
            ar = jnp.where(keep, pltpu.roll(at, d, 0), 1.0)
            br = jnp.where(keep, pltpu.roll(bt, d, 0), 0.0)
            bt = at * br + bt
            at = at * ar
        h = at * prev + bt
        h_scr[pl.ds(off, SUBLANES), :] = h
        return jnp.broadcast_to(h[SUBLANES - 1:SUBLANES, :], (SUBLANES, LRU_WIDTH))

    carry_scr[...] = lax.fori_loop(0, rows // SUBLANES, tile_body, carry_scr[...],
                                   unroll=LRU_UNROLL)

    y = h_scr[...] * _gelu_tanh(gate_ref[...])
    o_ref[...] = (_rms_rows(y) * gain_ref[...]).astype(BF16)


def _lru(x, gate, conv_w, conv_b, wax, bax, lam, gain, layer, batch):
    t = x.shape[0]
    nblk = t // batch // LRU_ROWS
    row = pl.BlockSpec((LRU_ROWS, LRU_WIDTH), lambda b, j: (b * nblk + j, 0))
    return pl.pallas_call(
        _lru_kernel,
        grid=(batch, nblk),
        in_specs=[row, row,
                  _resident((CONV_WIDTH, LRU_WIDTH), layer),
                  _resident((1, LRU_WIDTH), layer),
                  _resident((LRU_WIDTH, 2 * LRU_WIDTH), layer),
                  _resident((1, 2 * LRU_WIDTH), layer),
                  _resident((1, LRU_WIDTH), layer),
                  _resident((1, LRU_WIDTH), layer)],
        out_specs=row,
        out_shape=jax.ShapeDtypeStruct((t, LRU_WIDTH), BF16),
        scratch_shapes=[pltpu.VMEM((LRU_ROWS + SUBLANES, LRU_WIDTH), F32),
                        pltpu.VMEM((LRU_ROWS, LRU_WIDTH), F32),
                        pltpu.VMEM((LRU_ROWS, LRU_WIDTH), F32),
                        pltpu.VMEM((LRU_ROWS, LRU_WIDTH), F32),
                        pltpu.VMEM((SUBLANES, LRU_WIDTH), F32)],
        compiler_params=_params(32, 2),
        name="lru",
    )(x, gate, conv_w, conv_b, wax, bax, lam, gain)


def _ret_kernel(q_ref, kt_ref, v_ref, g_ref, cos_ref, sin_ref, cost_ref, sint_ref, decay_ref,
                zetat_ref, xi_ref, gamma_ref, gain_ref, o_ref, state_scr, y_scr):
    j = pl.program_id(1)

    @pl.when(j == 0)
    def _():
        state_scr[...] = jnp.zeros(state_scr.shape, F32)

    scale = RET_HEAD_DIM ** -0.5
    half = RET_HEAD_DIM // 2
    for c in range(q_ref.shape[0] // RET_CHUNK):
        rows = slice(c * RET_CHUNK, (c + 1) * RET_CHUNK)
        cs = cos_ref[rows, :]
        sn = sin_ref[rows, :]
        cst = cost_ref[:, rows]
        snt = sint_ref[:, rows]
        ssq = jnp.zeros((RET_CHUNK, 1), F32)
        for h in range(RET_HEADS):
            cols = slice(h * RET_HEAD_DIM, (h + 1) * RET_HEAD_DIM)
            qh = q_ref[rows, cols]
            kt = kt_ref[cols, rows]
            vh = v_ref[rows, cols].astype(BF16)
            qb = ((qh * cs + pltpu.roll(qh, half, 1) * sn) * scale).astype(BF16)
            kr = kt * cst + jnp.concatenate([kt[half:], kt[:half]], axis=0) * snt
            scores = _dot(qb, kr.astype(BF16)) * decay_ref[:, cols]
            inner = _dot(scores.astype(BF16), vh)
            st = state_scr[:, cols]
            cross = _dot(qb, st.astype(BF16)) * xi_ref[:, cols]
            kz = (kr * zetat_ref[cols, :]).astype(BF16)
            state_scr[:, cols] = gamma_ref[:, cols] * st + _dot(kz, vh)
            yh = _silu(g_ref[rows, cols]) * _rms_rows(inner + cross)
            ssq = ssq + jnp.sum(yh * yh, axis=-1, keepdims=True)
            y_scr[:, cols] = yh
        inv = lax.rsqrt(ssq * (1.0 / RET_WIDTH) + NORM_EPS)
        o_ref[rows, :] = ((y_scr[...] * inv) * gain_ref[...]).astype(BF16)


def _ret_tables(seq):
    dh, c, nh = RET_HEAD_DIM, RET_CHUNK, RET_HEADS
    pos = jnp.arange(seq, dtype=F32)
    inv_freq = ROPE_BASE ** (-jnp.arange(0, dh, 2, dtype=F32) / dh)
    ang = pos[:, None] * inv_freq[None, :]
    cos = jnp.cos(ang)
    sin = jnp.sin(ang)
    cos2 = jnp.concatenate([cos, cos], axis=-1)
    sin2 = jnp.concatenate([-sin, sin], axis=-1)
    log_gamma = jnp.log1p(-jnp.exp2(-5.0 - jnp.arange(nh, dtype=F32)))
    idx = jnp.arange(c, dtype=F32)
    diff = idx[:, None] - idx[None, :]
    decay = jnp.where(diff[None] >= 0,
                      jnp.exp(jnp.maximum(diff, 0.0)[None] * log_gamma[:, None, None]), 0.0)
    zeta = jnp.exp((c - 1.0 - idx)[None] * log_gamma[:, None])
    xi = jnp.exp((idx + 1.0)[None] * log_gamma[:, None])
    gamma_chunk = jnp.exp(c * log_gamma)
    decay_l = decay.transpose(1, 0, 2).reshape(c, nh * c)
    zeta_t = jnp.broadcast_to(zeta[:, None, :], (nh, dh, c)).reshape(nh * dh, c)
    xi_l = jnp.broadcast_to(xi.T[:, :, None], (c, nh, dh)).reshape(c, nh * dh)
    gamma_l = jnp.broadcast_to(gamma_chunk[:, None], (nh, dh)).reshape(1, nh * dh)
    return cos2, sin2, cos2.T, sin2.T, decay_l, zeta_t, xi_l, gamma_l


def _ret(q, k_t, v, g, tables, gain, layer, batch):
    t = q.shape[0]
    nblk = t // batch // RET_ROWS
    row = pl.BlockSpec((RET_ROWS, RET_WIDTH), lambda b, j: (b * nblk + j, 0))
    row_t = pl.BlockSpec((RET_WIDTH, RET_ROWS), lambda b, j: (0, b * nblk + j))
    rot = pl.BlockSpec((RET_ROWS, RET_HEAD_DIM), lambda b, j: (j, 0))
    rot_t = pl.BlockSpec((RET_HEAD_DIM, RET_ROWS), lambda b, j: (0, j))
    const = lambda shape: pl.BlockSpec(shape, lambda b, j: (0, 0), pipeline_mode=pl.Buffered(1))
    return pl.pallas_call(
        _ret_kernel,
        grid=(batch, nblk),
        in_specs=[row, row_t, row, row, rot, rot, rot_t, rot_t,
                  const((RET_CHUNK, RET_WIDTH)), const((RET_WIDTH, RET_CHUNK)),
                  const((RET_CHUNK, RET_WIDTH)), const((1, RET_WIDTH)),
                  _resident((1, RET_WIDTH), layer)],
        out_specs=row,
        out_shape=jax.ShapeDtypeStruct((t, RET_WIDTH), BF16),
        scratch_shapes=[pltpu.VMEM((RET_HEAD_DIM, RET_WIDTH), F32),
                        pltpu.VMEM((RET_CHUNK, RET_WIDTH), F32)],
        compiler_params=_params(32, 2),
        name="ret",
    )(q, k_t, v, g, *tables, gain)


def _block_diag(w):
    depth, n, d, e = w.shape
    eye = jnp.eye(n, dtype=w.dtype)
    return jnp.einsum('lnde,nm->lndme', w, eye).reshape(depth, n * d, n * e)


def kernel(x, ffn1_norm, ffn1_w_gate, ffn1_w_up, ffn1_w_down, mix_norm, w_in, s5_lambda_re, s5_lambda_im, s5_log_step, s5_b_re, s5_b_im, s5_c_re, s5_c_im, s5_d, s5_w_glu, s5_b_glu, s5_out_norm, ret_out_norm, lru_conv_w, lru_conv_b, lru_w_a, lru_b_a, lru_w_x, lru_b_x, lru_lambda, lru_out_norm, w_out, ffn2_norm, ffn2_w_gate, ffn2_w_up, ffn2_w_down, final_norm):
    batch, seq, d = x.shape
    depth = w_in.shape[0]
    t = batch * seq
    assert d == D_MODEL and seq % max(S5_STATE_ROWS, LRU_ROWS, RET_ROWS) == 0
    assert t % max(FFN_ROWS, PROJ_ROWS, S5_OUT_ROWS) == 0

    row3 = lambda a: a.reshape(depth, 1, a.shape[-1])
    bf = lambda a: a.astype(BF16)

    f1 = (row3(ffn1_norm), bf(ffn1_w_gate), bf(ffn1_w_up), bf(ffn1_w_down))
    f2 = (row3(ffn2_norm), bf(ffn2_w_gate), bf(ffn2_w_up), bf(ffn2_w_down))
    w_in_b = bf(w_in)
    k_off = sum(IN_SECTIONS[:K_SECTION])
    w_k_t = bf(jnp.swapaxes(w_in[:, :, k_off:k_off + RET_WIDTH], 1, 2))
    w_out_b = bf(w_out)
    s5_wst, s5_m, s5_vt, s5_consts = _s5_prepare(s5_lambda_re, s5_lambda_im, s5_log_step,
                                                 s5_b_re, s5_b_im, s5_c_re, s5_c_im)
    s5_glu = bf(s5_w_glu)
    lru_wax = bf(jnp.concatenate([_block_diag(lru_w_a), _block_diag(lru_w_x)], axis=-1))
    lru_bax = row3(jnp.concatenate([lru_b_a, lru_b_x], axis=-1))
    tables = _ret_tables(seq)

    xt = x.reshape(t, d)
    for l in range(depth):
        xt = _ffn(xt, *f1, l)
        u, q, k_t, v, g, xl, gl = _inproj(xt, row3(mix_norm), w_in_b, w_k_t, l)
        uc, sp = _s5_state(u, s5_wst, s5_consts, l, batch)
        y_s5 = _s5_out(u, uc, sp, s5_m, s5_vt, row3(s5_d), s5_glu, row3(s5_b_glu),
                       row3(s5_out_norm), l)
        y_ret = _ret(q, k_t, v, g, tables, row3(ret_out_norm), l, batch)
        y_lru = _lru(xl, gl, lru_conv_w, row3(lru_conv_b), lru_wax, lru_bax, row3(lru_lambda),
                     row3(lru_out_norm), l, batch)
        last = final_norm.reshape(1, d) if l == depth - 1 else None
        xt = _ffn(xt, *f2, l, mix=(y_s5, y_ret, y_lru, w_out_b), final_gain=last)
    return xt.reshape(batch, seq, d)
```

```python
import functools
import math

import jax
import jax.numpy as jnp
import numpy as np
from jax import lax
from jax.experimental import pallas as pl
from jax.experimental.pallas import tpu as pltpu

F32 = jnp.float32
BF16 = jnp.bfloat16

D_MODEL = 1024
D_FF = 2816
S5_WIDTH = 256
S5_GROUP_DIM = 16
S5_GROUPS = 16
S5_STATE = 64
S5_CH = S5_GROUPS * S5_STATE
RET_WIDTH = 512
RET_HEAD_DIM = 128
RET_HEADS = 4
RET_CHUNK = 128
LRU_WIDTH = 256
LRU_BLOCKS = 4
LRU_BLOCK_DIM = 64
CONV_WIDTH = 4
LRU_C = 8.0
ROPE_BASE = 10000.0
NORM_EPS = 1e-6
IN_SECTIONS = (S5_WIDTH, RET_WIDTH, RET_WIDTH, RET_WIDTH, RET_WIDTH, LRU_WIDTH, LRU_WIDTH)
IN_WIDTH = sum(IN_SECTIONS)

SUBLANES = 8
LANES = 128
MIB = 1024 * 1024

FFN_ROWS = 1024
FFN_CHUNK = 256
PROJ_ROWS = 512
S5_STATE_ROWS = 4096
S5_OUT_ROWS = 2048
LRU_ROWS = 1024
LRU_UNROLL = 4
RET_ROWS = 1024
RET_UNROLL = 2


def _params(vmem_mib, n_axes):
    return pltpu.CompilerParams(
        dimension_semantics=("arbitrary",) * n_axes,
        vmem_limit_bytes=vmem_mib * MIB)


def _rms_rows(x):
    return x * lax.rsqrt(jnp.mean(x * x, axis=-1, keepdims=True) + NORM_EPS)


def _gelu_tanh(x):
    c = math.sqrt(2.0 / math.pi)
    return x * (0.5 * (1.0 + jnp.tanh(c * (x + 0.044715 * (x * x * x)))))


def _sigmoid(x):
    return 0.5 + 0.5 * jnp.tanh(0.5 * x)


def _silu(x):
    h = 0.5 * x
    return h + h * jnp.tanh(h)


def _dot(a, b):
    return jnp.dot(a, b, preferred_element_type=F32)


def _resident(shape, layer):
    nd = len(shape)
    return pl.BlockSpec((None,) + tuple(shape), lambda *_: (layer,) + (0,) * nd,
                        pipeline_mode=pl.Buffered(1))


def _ffn_kernel(*refs, with_outproj, with_final_norm):
    refs = list(refs)
    o_ref = refs.pop()
    x = refs.pop(0)[...]
    if with_outproj:
        ys_ref, yr_ref, yl_ref, wo_ref = refs[:4]
        refs = refs[4:]
        x = x + _dot(ys_ref[...], wo_ref[0:S5_WIDTH, :])
        x = x + _dot(yr_ref[...], wo_ref[S5_WIDTH:S5_WIDTH + RET_WIDTH, :])
        x = x + _dot(yl_ref[...], wo_ref[S5_WIDTH + RET_WIDTH:, :])
    g_ref, wg_ref, wu_ref, wd_ref = refs[:4]
    h = (_rms_rows(x) * g_ref[...]).astype(BF16)
    acc = jnp.zeros(x.shape, F32)
    for c in range(D_FF // FFN_CHUNK):
        sl = slice(c * FFN_CHUNK, (c + 1) * FFN_CHUNK)
        act = (_silu(_dot(h, wg_ref[:, sl])) * _dot(h, wu_ref[:, sl])).astype(BF16)
        acc = acc + _dot(act, wd_ref[sl, :])
    y = x + 0.5 * acc
    if with_final_norm:
        y = _rms_rows(y) * refs[4][...]
    o_ref[...] = y


def _ffn(x, gain, wg, wu, wd, layer, mix=None, final_gain=None):
    t = x.shape[0]
    spec = lambda w: pl.BlockSpec((FFN_ROWS, w), lambda i: (i, 0))
    operands = [x]
    in_specs = [spec(D_MODEL)]
    if mix is not None:
        y_s5, y_ret, y_lru, w_out = mix
        operands += [y_s5, y_ret, y_lru, w_out]
        in_specs += [spec(S5_WIDTH), spec(RET_WIDTH), spec(LRU_WIDTH),
                     _resident((D_MODEL, D_MODEL), layer)]
    operands += [gain, wg, wu, wd]
    in_specs += [_resident((1, D_MODEL), layer),
                 _resident((D_MODEL, D_FF), layer),
                 _resident((D_MODEL, D_FF), layer),
                 _resident((D_FF, D_MODEL), layer)]
    if final_gain is not None:
        operands.append(final_gain)
        in_specs.append(pl.BlockSpec((1, D_MODEL), lambda i: (0, 0)))
    return pl.pallas_call(
        functools.partial(_ffn_kernel, with_outproj=mix is not None,
                          with_final_norm=final_gain is not None),
        grid=(t // FFN_ROWS,),
        in_specs=in_specs,
        out_specs=spec(D_MODEL),
        out_shape=jax.ShapeDtypeStruct(x.shape, F32),
        compiler_params=_params(56 if mix is not None else 50, 1),
        name="ffn_mix" if mix is not None else "ffn",
    )(*operands)


K_SECTION = 2


def _inproj_kernel(x_ref, g_ref, w_ref, wkt_ref, *out_refs):
    h = (_rms_rows(x_ref[...]) * g_ref[...]).astype(BF16)
    off = 0
    for n, (o_ref, width) in enumerate(zip(out_refs, IN_SECTIONS)):
        if n == K_SECTION:
            o_ref[...] = lax.dot_general(wkt_ref[...], h, (((1,), (1,)), ((), ())),
                                         preferred_element_type=F32)
        else:
            o_ref[...] = _dot(h, w_ref[:, off:off + width])
        off += width


def _inproj(x, gain, w_in, w_k_t, layer):
    t = x.shape[0]
    row = lambda w: pl.BlockSpec((PROJ_ROWS, w), lambda i: (i, 0))
    out_specs = [row(w) for w in IN_SECTIONS]
    out_shape = [jax.ShapeDtypeStruct((t, w), F32) for w in IN_SECTIONS]
    out_specs[K_SECTION] = pl.BlockSpec((RET_WIDTH, PROJ_ROWS), lambda i: (0, i))
    out_shape[K_SECTION] = jax.ShapeDtypeStruct((RET_WIDTH, t), F32)
    return pl.pallas_call(
        _inproj_kernel,
        grid=(t // PROJ_ROWS,),
        in_specs=[row(D_MODEL),
                  _resident((1, D_MODEL), layer),
                  _resident((D_MODEL, IN_WIDTH), layer),
                  _resident((RET_WIDTH, D_MODEL), layer)],
        out_specs=out_specs,
        out_shape=out_shape,
        compiler_params=_params(40, 1),
        name="inproj",
    )(x, gain, w_in, w_k_t)


S5_CHUNK = 8
S5_CAT = S5_CHUNK * S5_WIDTH
S5_CONST_ROWS = 8 * SUBLANES


def _cmul(a, b):
    return a[0] * b[0] - a[1] * b[1], a[0] * b[1] + a[1] * b[0]


def _s5_prep_kernel(lr_ref, li_ref, ls_ref, er_ref, ei_ref, ctr_ref, cti_ref, cst_ref,
                    wst_ref, kk_ref, vt_ref, c_ref):
    k = pl.program_id(1)
    lr = lr_ref[...]
    li = li_ref[...]
    step = jnp.exp(ls_ref[...])
    ar = lr * step
    ai = li * step
    mag = jnp.exp(ar)
    p1 = (mag * jnp.cos(ai), mag * jnp.sin(ai))
    nr = p1[0] - 1.0
    den = lr * lr + li * li
    f = ((nr * lr + p1[1] * li) / den, (p1[1] * lr - nr * li) / den)
    powers = [(jnp.ones_like(lr), jnp.zeros_like(lr)), p1]
    for _ in range(2, S5_CHUNK + 1):
        powers.append(_cmul(powers[-1], p1))

    def pick(n):
        out = powers[0]
        for i in range(1, S5_CHUNK + 1):
            out = (jnp.where(n == i, powers[i][0], out[0]), jnp.where(n == i, powers[i][1], out[1]))
        return out

    fp = _cmul(f, pick(k))
    er = er_ref[...]
    ei = ei_ref[...]
    w = jnp.concatenate([fp[0] * er - fp[1] * ei, fp[0] * ei + fp[1] * er], axis=1)
    wb = w.astype(BF16)
    wst_ref[...] = wb
    kk_ref[...] = _dot(wb, cst_ref[...])
    a, b = pick(k + 1)
    ctr = ctr_ref[...]
    cti = cti_ref[...]
    vt_ref[:, :S5_CH] = (ctr * a - cti * b).astype(BF16)
    vt_ref[:, S5_CH:] = (-(ctr * b) - cti * a).astype(BF16)

    @pl.when(k == 0)
    def _():
        q = [powers[S5_CHUNK]]
        for _ in range(1, SUBLANES):
            q.append(_cmul(q[-1], q[0]))
        row = lax.broadcasted_iota(jnp.int32, (SUBLANES, S5_CH), 0)
        zero = jnp.zeros((SUBLANES, S5_CH), F32)
        for s, d in enumerate((1, 2, 4)):
            for part in range(2):
                base = (2 * s + part) * SUBLANES
                c_ref[base:base + SUBLANES, :] = jnp.where(
                    row >= d, jnp.broadcast_to(q[d - 1][part], (SUBLANES, S5_CH)), zero)
        for part in range(2):
            tile = zero
            for r in range(SUBLANES):
                tile = jnp.where(row == r, jnp.broadcast_to(q[r][part], (SUBLANES, S5_CH)), tile)
            base = (6 + part) * SUBLANES
            c_ref[base:base + SUBLANES, :] = tile


def _s5_prepare(lam_re, lam_im, log_step, b_re, b_im, c_re, c_im):
    depth = lam_re.shape[0]
    def embed(a):
        n, m = a.shape[2:]
        same = np.equal.outer(np.repeat(np.arange(S5_GROUPS), n), np.arange(S5_GROUPS))
        tiled = jnp.broadcast_to(a.reshape(depth, S5_GROUPS * n, 1, m),
                                 (depth, S5_GROUPS * n, S5_GROUPS, m))
        return jnp.where(same[None, :, :, None], tiled, 0.0).reshape(
            depth, S5_GROUPS * n, S5_GROUPS * m)

    b_rows = lambda b: embed(jnp.swapaxes(b, 2, 3))
    c_rows = embed
    c_cols = lambda c: embed(jnp.swapaxes(c, 2, 3))
    c_stack = jnp.concatenate([c_cols(c_re), -c_cols(c_im)], axis=1).astype(BF16)
    vec = lambda a: a.reshape(depth, 1, S5_CH)
    ls = jnp.broadcast_to(log_step[:, :, None], (depth, S5_GROUPS, S5_STATE))
    vspec = pl.BlockSpec((None, 1, S5_CH), lambda l, k: (l, 0, 0))
    mspec = pl.BlockSpec((None, S5_WIDTH, S5_CH), lambda l, k: (l, 0, 0))
    wst, kk, vt, consts = pl.pallas_call(
        _s5_prep_kernel,
        grid=(depth, S5_CHUNK),
        in_specs=[vspec, vspec, vspec, mspec, mspec, mspec, mspec,
                  pl.BlockSpec((None, 2 * S5_CH, S5_WIDTH), lambda l, k: (l, 0, 0))],
        out_specs=[pl.BlockSpec((None, S5_WIDTH, 2 * S5_CH),
                                lambda l, k: (l, S5_CHUNK - 1 - k, 0)),
                   pl.BlockSpec((None, None, S5_WIDTH, S5_WIDTH), lambda l, k: (l, k, 0, 0)),
                   pl.BlockSpec((None, S5_WIDTH, 2 * S5_CH), lambda l, k: (l, k, 0)),
                   pl.BlockSpec((None, S5_CONST_ROWS, S5_CH), lambda l, k: (l, 0, 0))],
        out_shape=[jax.ShapeDtypeStruct((depth, S5_CAT, 2 * S5_CH), BF16),
                   jax.ShapeDtypeStruct((depth, S5_CHUNK, S5_WIDTH, S5_WIDTH), F32),
                   jax.ShapeDtypeStruct((depth, S5_CAT, 2 * S5_CH), BF16),
                   jax.ShapeDtypeStruct((depth, S5_CONST_ROWS, S5_CH), F32)],
        compiler_params=_params(40, 2),
        name="s5_prep",
    )(vec(lam_re), vec(lam_im), vec(ls), b_rows(b_re), b_rows(b_im), c_rows(c_re), c_rows(c_im),
      c_stack)
    zero = jnp.zeros((depth, S5_WIDTH, S5_WIDTH), F32)
    m = jnp.concatenate(
        [jnp.concatenate([zero] * j + [kk[:, k] for k in range(S5_CHUNK - j)], axis=-1)
         for j in range(S5_CHUNK)], axis=-2).astype(BF16)
    return wst, m, vt, consts


def _s5_state_kernel(ulo_ref, uhi_ref, wst_ref, k_ref, uc_ref, sp_ref, ds_scr, s_scr):
    j = pl.program_id(1)
    nch = ulo_ref.shape[0] // S5_CHUNK
    ncol = S5_CH // LANES
    cols = [(slice(cb * LANES, (cb + 1) * LANES),
             slice(S5_CH + cb * LANES, S5_CH + (cb + 1) * LANES)) for cb in range(ncol)]

    @pl.when(j == 0)
    def _():
        s_scr[0:SUBLANES, :] = jnp.zeros((SUBLANES, 2 * S5_CH), F32)

    @pl.when(j > 0)
    def _():
        s_scr[0:SUBLANES, :] = s_scr[nch:nch + SUBLANES, :]

    uc = jnp.concatenate([half[pl.ds(jj, nch, stride=S5_CHUNK), :]
                          for jj in range(S5_CHUNK) for half in (ulo_ref, uhi_ref)],
                         axis=1).astype(BF16)
    uc_ref[...] = uc
    ds_scr[...] = _dot(uc, wst_ref[...])

    def tile_body(i, carry):
        off = pl.multiple_of(i * SUBLANES, SUBLANES)
        loaded = [(ds_scr[pl.ds(off, SUBLANES), re], ds_scr[pl.ds(off, SUBLANES), im])
                  for re, im in cols]
        done = []
        for cb, (br, bi) in enumerate(loaded):
            re = cols[cb][0]
            for s, d in enumerate((1, 2, 4)):
                cr = k_ref[(2 * s) * SUBLANES:(2 * s + 1) * SUBLANES, re]
                ci = k_ref[(2 * s + 1) * SUBLANES:(2 * s + 2) * SUBLANES, re]
                rr = pltpu.roll(br, d, 0)
                ri = pltpu.roll(bi, d, 0)
                br, bi = br + (cr * rr - ci * ri), bi + (cr * ri + ci * rr)
            pr, pi = carry[cb], carry[ncol + cb]
            wr = k_ref[6 * SUBLANES:7 * SUBLANES, re]
            wi = k_ref[7 * SUBLANES:8 * SUBLANES, re]
            done.append((br + (wr * pr - wi * pi), bi + (wr * pi + wi * pr)))
        for (re, im), (sr, si) in zip(cols, done):
            s_scr[pl.ds(off + SUBLANES, SUBLANES), re] = sr
            s_scr[pl.ds(off + SUBLANES, SUBLANES), im] = si
        return tuple(last(sr) for sr, _ in done) + tuple(last(si) for _, si in done)

    last = lambda a: jnp.broadcast_to(a[SUBLANES - 1:SUBLANES, :], (SUBLANES, LANES))
    init = (tuple(last(s_scr[0:SUBLANES, re]) for re, _ in cols)
            + tuple(last(s_scr[0:SUBLANES, im]) for _, im in cols))
    lax.fori_loop(0, nch // SUBLANES, tile_body, init)

    sp_ref[...] = s_scr[SUBLANES - 1:SUBLANES - 1 + nch, :].astype(BF16)


def _s5_state(u, wst, consts, layer, batch):
    t = u.shape[0]
    nblk = t // batch // S5_STATE_ROWS
    nch = S5_STATE_ROWS // S5_CHUNK
    chunk_rows = pl.BlockSpec((nch, S5_CAT), lambda b, j: (b * nblk + j, 0))
    return pl.pallas_call(
        _s5_state_kernel,
        grid=(batch, nblk),
        in_specs=[pl.BlockSpec((S5_STATE_ROWS, LANES), lambda b, j: (b * nblk + j, 0)),
                  pl.BlockSpec((S5_STATE_ROWS, LANES), lambda b, j: (b * nblk + j, 1)),
                  _resident((S5_CAT, 2 * S5_CH), layer),
                  _resident((S5_CONST_ROWS, S5_CH), layer)],
        out_specs=[chunk_rows, chunk_rows],
        out_shape=[jax.ShapeDtypeStruct((t // S5_CHUNK, S5_CAT), BF16),
                   jax.ShapeDtypeStruct((t // S5_CHUNK, 2 * S5_CH), BF16)],
        scratch_shapes=[pltpu.VMEM((nch, 2 * S5_CH), F32),
                        pltpu.VMEM((nch + SUBLANES, 2 * S5_CH), F32)],
        compiler_params=_params(44, 2),
        name="s5_state",
    )(u, u, wst, consts)


def _s5_out_kernel(u_ref, uc_ref, sp_ref, m_ref, vt_ref, d_ref, wglu_ref, bglu_ref, gain_ref,
                   o_ref, ylo_scr, yhi_scr):
    nch = uc_ref.shape[0]
    sp = sp_ref[...]
    for r in range(S5_CHUNK):
        cols = slice(r * S5_WIDTH, (r + 1) * S5_WIDTH)
        used = (r + 1) * S5_WIDTH
        y_r = _dot(uc_ref[:, :used], m_ref[:used, cols])
        y_r = y_r + lax.dot_general(sp, vt_ref[cols, :], (((1,), (1,)), ((), ())),
                                    preferred_element_type=F32)
        ylo_scr[pl.ds(r, nch, stride=S5_CHUNK), :] = y_r[:, :LANES]
        yhi_scr[pl.ds(r, nch, stride=S5_CHUNK), :] = y_r[:, LANES:]
    y = jnp.concatenate([ylo_scr[...], yhi_scr[...]], axis=1) + d_ref[...] * u_ref[...]
    y = _gelu_tanh(y)
    y = y * _sigmoid(_dot(y.astype(BF16), wglu_ref[...]) + bglu_ref[...])
    o_ref[...] = (_rms_rows(y) * gain_ref[...]).astype(BF16)


def _s5_out(u, uc, sp, m, vt, d_skip, w_glu, b_glu, gain, layer):
    t = u.shape[0]
    nch = S5_OUT_ROWS // S5_CHUNK
    row = pl.BlockSpec((S5_OUT_ROWS, S5_WIDTH), lambda i: (i, 0))
    chunk_rows = pl.BlockSpec((nch, S5_CAT), lambda i: (i, 0))
    return pl.pallas_call(
        _s5_out_kernel,
        grid=(t // S5_OUT_ROWS,),
        in_specs=[row, chunk_rows, chunk_rows,
                  _resident((S5_CAT, S5_CAT), layer),
                  _resident((S5_CAT, 2 * S5_CH), layer),
                  _resident((1, S5_WIDTH), layer),
                  _resident((S5_WIDTH, S5_WIDTH), layer),
                  _resident((1, S5_WIDTH), layer),
                  _resident((1, S5_WIDTH), layer)],
        out_specs=row,
        out_shape=jax.ShapeDtypeStruct((t, S5_WIDTH), BF16),
        scratch_shapes=[pltpu.VMEM((S5_OUT_ROWS, LANES), F32),
                        pltpu.VMEM((S5_OUT_ROWS, LANES), F32)],
        compiler_params=_params(48, 1),
        name="s5_out",
    )(u, uc, sp, m, vt, d_skip, w_glu, b_glu, gain)


def _lru_kernel(x_ref, gate_ref, cw_ref, cb_ref, wax_ref, bax_ref, lam_ref, gain_ref, o_ref,
                ext_scr, a_scr, b_scr, h_scr, carry_scr):
    j = pl.program_id(1)
    rows = x_ref.shape[0]
    pad = SUBLANES

    @pl.when(j == 0)
    def _():
        ext_scr[0:pad, :] = jnp.zeros((pad, LRU_WIDTH), F32)
        carry_scr[...] = jnp.zeros(carry_scr.shape, F32)

    @pl.when(j > 0)
    def _():
        ext_scr[0:pad, :] = ext_scr[rows:rows + pad, :]

    ext_scr[pad:rows + pad, :] = x_ref[...]
    xc = cb_ref[...] + cw_ref[CONV_WIDTH - 1:CONV_WIDTH, :] * ext_scr[pad:rows + pad, :]
    for lag in range(1, CONV_WIDTH):
        tap = cw_ref[CONV_WIDTH - 1 - lag:CONV_WIDTH - lag, :]
        xc = xc + tap * ext_scr[pad - lag:rows + pad - lag, :]

    gates = _sigmoid(_dot(xc.astype(BF16), wax_ref[...]) + bax_ref[...])
    r = gates[:, :LRU_WIDTH]
    i = gates[:, LRU_WIDTH:]
    z = -lam_ref[...]
    softplus = jnp.maximum(z, 0.0) + jnp.log1p(jnp.exp(-jnp.abs(z)))
    a = jnp.exp((-LRU_C) * r * softplus)
    a_scr[...] = a
    b_scr[...] = jnp.sqrt(1.0 - a * a) * (i * xc)

    row = lax.broadcasted_iota(jnp.int32, (SUBLANES, LRU_WIDTH), 0)

    def tile_body(t, prev):
        off = pl.multiple_of(t * SUBLANES, SUBLANES)
        at = a_scr[pl.ds(off, SUBLANES), :]
        bt = b_scr[pl.ds(off, SUBLANES), :]
        for d in (1, 2, 4):
            keep = row >= d
            ar = jnp.where(keep, pltpu.roll(at, d, 0), 1.0)
            br = jnp.where(keep, pltpu.roll(bt, d, 0), 0.0)
            bt = at * br + bt
            at = at * ar
        h = at * prev + bt
        h_scr[pl.ds(off, SUBLANES), :] = h
        return jnp.broadcast_to(h[SUBLANES - 1:SUBLANES, :], (SUBLANES, LRU_WIDTH))

    carry_scr[...] = lax.fori_loop(0, rows // SUBLANES, tile_body, carry_scr[...],
                                   unroll=LRU_UNROLL)

    y = h_scr[...] * _gelu_tanh(gate_ref[...])
    o_ref[...] = (_rms_rows(y) * gain_ref[...]).astype(BF16)


def _lru(x, gate, conv_w, conv_b, wax, bax, lam, gain, layer, batch):
    t = x.shape[0]
    nblk = t // batch // LRU_ROWS
    row = pl.BlockSpec((LRU_ROWS, LRU_WIDTH), lambda b, j: (b * nblk + j, 0))
    return pl.pallas_call(
        _lru_kernel,
        grid=(batch, nblk),
        in_specs=[row, row,
                  _resident((CONV_WIDTH, LRU_WIDTH), layer),
                  _resident((1, LRU_WIDTH), layer),
                  _resident((LRU_WIDTH, 2 * LRU_WIDTH), layer),
                  _resident((1, 2 * LRU_WIDTH), layer),
                  _resident((1, LRU_WIDTH), layer),
                  _resident((1, LRU_WIDTH), layer)],
        out_specs=row,
        out_shape=jax.ShapeDtypeStruct((t, LRU_WIDTH), BF16),
        scratch_shapes=[pltpu.VMEM((LRU_ROWS + SUBLANES, LRU_WIDTH), F32),
                        pltpu.VMEM((LRU_ROWS, LRU_WIDTH), F32),
                        pltpu.VMEM((LRU_ROWS, LRU_WIDTH), F32),
                        pltpu.VMEM((LRU_ROWS, LRU_WIDTH), F32),
                        pltpu.VMEM((SUBLANES, LRU_WIDTH), F32)],
        compiler_params=_params(32, 2),
        name="lru",
    )(x, gate, conv_w, conv_b, wax, bax, lam, gain)


def _ret_kernel(q_ref, kt_ref, v_ref, g_ref, cos_ref, sin_ref, cost_ref, sint_ref, decay_ref,
                zetat_ref, xi_ref, gamma_ref, gain_ref, o_ref, state_scr, y_scr):
    j = pl.program_id(1)

    @pl.when(j == 0)
    def _():
        state_scr[...] = jnp.zeros(state_scr.shape, F32)

    half = RET_HEAD_DIM // 2

    def chunk(start):
        rows = pl.ds(start, RET_CHUNK)
        cs = cos_ref[rows, :]
        sn = sin_ref[rows, :]
        cst = cost_ref[:, rows]
        snt = sint_ref[:, rows]
        ssq = jnp.zeros((RET_CHUNK, 1), F32)
        for h in range(RET_HEADS):
            cols = slice(h * RET_HEAD_DIM, (h + 1) * RET_HEAD_DIM)
            qh = q_ref[rows, cols]
            kt = kt_ref[cols, rows]
            vh = v_ref[rows, cols].astype(BF16)
            qb = (qh * cs + pltpu.roll(qh, half, 1) * sn).astype(BF16)
            kr = kt * cst + jnp.concatenate([kt[half:], kt[:half]], axis=0) * snt
            scores = _dot(qb, kr.astype(BF16)) * decay_ref[:, cols]
            inner = _dot(scores.astype(BF16), vh)
            st = state_scr[:, cols]
            cross = _dot(qb, st.astype(BF16)) * xi_ref[:, cols]
            kz = (kr * zetat_ref[cols, :]).astype(BF16)
            state_scr[:, cols] = gamma_ref[:, cols] * st + _dot(kz, vh)
            yh = _silu(g_ref[rows, cols]) * _rms_rows(inner + cross)
            ssq = ssq + jnp.sum(yh * yh, axis=-1, keepdims=True)
            y_scr[:, cols] = yh
        inv = lax.rsqrt(ssq * (1.0 / RET_WIDTH) + NORM_EPS)
        o_ref[rows, :] = ((y_scr[...] * inv) * gain_ref[...]).astype(BF16)

    def body(i, carry):
        for c in range(RET_UNROLL):
            chunk(pl.multiple_of((i * RET_UNROLL + c) * RET_CHUNK, RET_CHUNK))
        return carry

    lax.fori_loop(0, q_ref.shape[0] // (RET_CHUNK * RET_UNROLL), body, 0)


def _ret_tables(seq):
    dh, c, nh = RET_HEAD_DIM, RET_CHUNK, RET_HEADS
    pos = np.arange(seq, dtype=np.float64)
    inv_freq = ROPE_BASE ** (-np.arange(0, dh, 2, dtype=np.float64) / dh)
    ang = pos[:, None] * inv_freq[None, :]
    cos = np.cos(ang)
    sin = np.sin(ang)
    cos2 = np.concatenate([cos, cos], axis=-1)
    sin2 = np.concatenate([-sin, sin], axis=-1)
    log_gamma = np.log1p(-np.exp2(-5.0 - np.arange(nh, dtype=np.float64)))
    idx = np.arange(c, dtype=np.float64)
    diff = idx[:, None] - idx[None, :]
    decay = np.where(diff[None] >= 0,
                     np.exp(np.maximum(diff, 0.0)[None] * log_gamma[:, None, None]), 0.0)
    zeta = np.exp((c - 1.0 - idx)[None] * log_gamma[:, None])
    xi = np.exp((idx + 1.0)[None] * log_gamma[:, None])
    gamma_chunk = np.exp(c * log_gamma)
    decay_l = decay.transpose(1, 0, 2).reshape(c, nh * c)
    zeta_t = np.broadcast_to(zeta[:, None, :], (nh, dh, c)).reshape(nh * dh, c)
    xi_l = np.broadcast_to(xi.T[:, :, None], (c, nh, dh)).reshape(c, nh * dh)
    gamma_l = np.broadcast_to(gamma_chunk[:, None], (nh, dh)).reshape(1, nh * dh)
    scale = dh ** -0.5
    tables = (cos2 * scale, sin2 * scale, cos2.T, sin2.T, decay_l, zeta_t, xi_l, gamma_l)
    return tuple(jnp.asarray(np.ascontiguousarray(a, dtype=np.float32)) for a in tables)


def _ret(q, k_t, v, g, tables, gain, layer, batch):
    t = q.shape[0]
    nblk = t // batch // RET_ROWS
    row = pl.BlockSpec((RET_ROWS, RET_WIDTH), lambda b, j: (b * nblk + j, 0))
    row_t = pl.BlockSpec((RET_WIDTH, RET_ROWS), lambda b, j: (0, b * nblk + j))
    rot = pl.BlockSpec((RET_ROWS, RET_HEAD_DIM), lambda b, j: (j, 0))
    rot_t = pl.BlockSpec((RET_HEAD_DIM, RET_ROWS), lambda b, j: (0, j))
    const = lambda shape: pl.BlockSpec(shape, lambda b, j: (0, 0), pipeline_mode=pl.Buffered(1))
    return pl.pallas_call(
        _ret_kernel,
        grid=(batch, nblk),
        in_specs=[row, row_t, row, row, rot, rot, rot_t, rot_t,
                  const((RET_CHUNK, RET_WIDTH)), const((RET_WIDTH, RET_CHUNK)),
                  const((RET_CHUNK, RET_WIDTH)), const((1, RET_WIDTH)),
                  _resident((1, RET_WIDTH), layer)],
        out_specs=row,
        out_shape=jax.ShapeDtypeStruct((t, RET_WIDTH), BF16),
        scratch_shapes=[pltpu.VMEM((RET_HEAD_DIM, RET_WIDTH), F32),
                        pltpu.VMEM((RET_CHUNK, RET_WIDTH), F32)],
        compiler_params=_params(32, 2),
        name="ret",
    )(q, k_t, v, g, *tables, gain)


def _block_diag(w):
    depth, n, d, e = w.shape
    eye = jnp.eye(n, dtype=w.dtype)
    return jnp.einsum('lnde,nm->lndme', w, eye).reshape(depth, n * d, n * e)


def kernel(x, ffn1_norm, ffn1_w_gate, ffn1_w_up, ffn1_w_down, mix_norm, w_in, s5_lambda_re, s5_lambda_im, s5_log_step, s5_b_re, s5_b_im, s5_c_re, s5_c_im, s5_d, s5_w_glu, s5_b_glu, s5_out_norm, ret_out_norm, lru_conv_w, lru_conv_b, lru_w_a, lru_b_a, lru_w_x, lru_b_x, lru_lambda, lru_out_norm, w_out, ffn2_norm, ffn2_w_gate, ffn2_w_up, ffn2_w_down, final_norm):
    batch, seq, d = x.shape
    depth = w_in.shape[0]
    t = batch * seq
    assert d == D_MODEL and seq % max(S5_STATE_ROWS, LRU_ROWS, RET_ROWS) == 0
    assert t % max(FFN_ROWS, PROJ_ROWS, S5_OUT_ROWS) == 0

    row3 = lambda a: a.reshape(depth, 1, a.shape[-1])
    bf = lambda a: a.astype(BF16)

    f1 = (row3(ffn1_norm), bf(ffn1_w_gate), bf(ffn1_w_up), bf(ffn1_w_down))
    f2 = (row3(ffn2_norm), bf(ffn2_w_gate), bf(ffn2_w_up), bf(ffn2_w_down))
    w_in_b = lax.optimization_barrier(bf(w_in))
    k_off = sum(IN_SECTIONS[:K_SECTION])
    w_k_t = jnp.swapaxes(w_in_b[:, :, k_off:k_off + RET_WIDTH], 1, 2)
    w_out_b = bf(w_out)
    s5_wst, s5_m, s5_vt, s5_consts = _s5_prepare(s5_lambda_re, s5_lambda_im, s5_log_step,
                                                 s5_b_re, s5_b_im, s5_c_re, s5_c_im)
    s5_glu = bf(s5_w_glu)
    lru_wax = bf(jnp.concatenate([_block_diag(lru_w_a), _block_diag(lru_w_x)], axis=-1))
    lru_bax = row3(jnp.concatenate([lru_b_a, lru_b_x], axis=-1))
    tables = _ret_tables(seq)

    xt = x.reshape(t, d)
    for l in range(depth):
        xt = _ffn(xt, *f1, l)
        u, q, k_t, v, g, xl, gl = _inproj(xt, row3(mix_norm), w_in_b, w_k_t, l)
        uc, sp = _s5_state(u, s5_wst, s5_consts, l, batch)
        y_s5 = _s5_out(u, uc, sp, s5_m, s5_vt, row3(s5_d), s5_glu, row3(s5_b_glu),
                       row3(s5_out_norm), l)
        y_ret = _ret(q, k_t, v, g, tables, row3(ret_out_norm), l, batch)
        y_lru = _lru(xl, gl, lru_conv_w, row3(lru_conv_b), lru_wax, lru_bax, row3(lru_lambda),
                     row3(lru_out_norm), l, batch)
        last = final_norm.reshape(1, d) if l == depth - 1 else None
        xt = _ffn(xt, *f2, l, mix=(y_s5, y_ret, y_lru, w_out_b), final_gain=last)
    return xt.reshape(batch, seq, d)
```

```python
import functools
import math

import jax
import jax.numpy as jnp
import numpy as np
from jax import lax
from jax.experimental import pallas as pl
from jax.experimental.pallas import tpu as pltpu

F32 = jnp.float32
BF16 = jnp.bfloat16

D_MODEL = 1024
D_FF = 2816
S5_WIDTH = 256
S5_GROUP_DIM = 16
S5_GROUPS = 16
S5_STATE = 64
S5_CH = S5_GROUPS * S5_STATE
RET_WIDTH = 512
RET_HEAD_DIM = 128
RET_HEADS = 4
RET_CHUNK = 128
LRU_WIDTH = 256
LRU_BLOCKS = 4
LRU_BLOCK_DIM = 64
CONV_WIDTH = 4
LRU_C = 8.0
ROPE_BASE = 10000.0
NORM_EPS = 1e-6
IN_SECTIONS = (S5_WIDTH, RET_WIDTH, RET_WIDTH, RET_WIDTH, RET_WIDTH, LRU_WIDTH, LRU_WIDTH)
IN_WIDTH = sum(IN_SECTIONS)

SUBLANES = 8
LANES = 128
MIB = 1024 * 1024

FFN_ROWS = 1024
FFN_CHUNK = 256
PROJ_ROWS = 512
S5_STATE_ROWS = 4096
S5_OUT_ROWS = 2048
LRU_ROWS = 1024
LRU_UNROLL = 4
RET_ROWS = 1024
RET_UNROLL = 2


def _params(vmem_mib, n_axes):
    return pltpu.CompilerParams(
        dimension_semantics=("arbitrary",) * n_axes,
        vmem_limit_bytes=vmem_mib * MIB)


def _rms_rows(x):
    return x * lax.rsqrt(jnp.mean(x * x, axis=-1, keepdims=True) + NORM_EPS)


def _gelu_tanh(x):
    c = math.sqrt(2.0 / math.pi)
    return x * (0.5 * (1.0 + jnp.tanh(c * (x + 0.044715 * (x * x * x)))))


def _sigmoid(x):
    return 0.5 + 0.5 * jnp.tanh(0.5 * x)


def _silu(x):
    h = 0.5 * x
    return h + h * jnp.tanh(h)


def _dot(a, b):
    return jnp.dot(a, b, preferred_element_type=F32)


def _resident(shape, layer):
    nd = len(shape)
    return pl.BlockSpec((None,) + tuple(shape), lambda *_: (layer,) + (0,) * nd,
                        pipeline_mode=pl.Buffered(1))


def _stream_cast(sources, targets, stage, sem):
    copy = lambda k: pltpu.make_async_copy(sources[k], stage.at[k % 2], sem.at[k % 2])
    copy(0).start()
    for k in range(len(sources)):
        if k + 1 < len(sources):
            copy(k + 1).start()
        copy(k).wait()
        ref, idx = targets[k]
        ref[idx] = stage[k % 2].astype(BF16)


def _ffn_kernel(*refs, layer, with_outproj, with_final_norm):
    refs = list(refs)
    wg_s, wu_s, wd_s, stage_c, stage_r, sem_c, sem_r = refs[-7:]
    o_ref = refs[-8]
    refs = refs[:-8]
    x_ref = refs.pop(0)
    if with_outproj:
        ys_ref, yr_ref, yl_ref, wo_ref = refs[:4]
        refs = refs[4:]
    g_ref, wg_hbm, wu_hbm, wd_hbm = refs[:4]

    @pl.when(pl.program_id(0) == 0)
    def _():
        chunks = [slice(c * FFN_CHUNK, (c + 1) * FFN_CHUNK) for c in range(D_FF // FFN_CHUNK)]
        _stream_cast([w.at[layer, :, sl] for sl in chunks for w in (wg_hbm, wu_hbm)],
                     [(w, (slice(None), sl)) for sl in chunks for w in (wg_s, wu_s)],
                     stage_c, sem_c)
        _stream_cast([wd_hbm.at[layer, sl, :] for sl in chunks],
                     [(wd_s, (sl, slice(None))) for sl in chunks], stage_r, sem_r)

    x = x_ref[...]
    if with_outproj:
        x = x + _dot(ys_ref[...], wo_ref[0:S5_WIDTH, :])
        x = x + _dot(yr_ref[...], wo_ref[S5_WIDTH:S5_WIDTH + RET_WIDTH, :])
        x = x + _dot(yl_ref[...], wo_ref[S5_WIDTH + RET_WIDTH:, :])
    h = (_rms_rows(x) * g_ref[...]).astype(BF16)
    acc = jnp.zeros(x.shape, F32)
    for c in range(D_FF // FFN_CHUNK):
        sl = slice(c * FFN_CHUNK, (c + 1) * FFN_CHUNK)
        act = (_silu(_dot(h, wg_s[:, sl])) * _dot(h, wu_s[:, sl])).astype(BF16)
        acc = acc + _dot(act, wd_s[sl, :])
    y = x + 0.5 * acc
    if with_final_norm:
        y = _rms_rows(y) * refs[4][...]
    o_ref[...] = y


def _ffn(x, gain, wg, wu, wd, layer, mix=None, final_gain=None):
    t = x.shape[0]
    spec = lambda w: pl.BlockSpec((FFN_ROWS, w), lambda i: (i, 0))
    hbm = pl.BlockSpec(memory_space=pl.ANY)
    operands = [x]
    in_specs = [spec(D_MODEL)]
    if mix is not None:
        y_s5, y_ret, y_lru, w_out = mix
        operands += [y_s5, y_ret, y_lru, w_out]
        in_specs += [spec(S5_WIDTH), spec(RET_WIDTH), spec(LRU_WIDTH),
                     _resident((D_MODEL, D_MODEL), layer)]
    operands += [gain, wg, wu, wd]
    in_specs += [_resident((1, D_MODEL), layer), hbm, hbm, hbm]
    if final_gain is not None:
        operands.append(final_gain)
        in_specs.append(pl.BlockSpec((1, D_MODEL), lambda i: (0, 0)))
    return pl.pallas_call(
        functools.partial(_ffn_kernel, layer=layer, with_outproj=mix is not None,
                          with_final_norm=final_gain is not None),
        grid=(t // FFN_ROWS,),
        in_specs=in_specs,
        out_specs=spec(D_MODEL),
        out_shape=jax.ShapeDtypeStruct(x.shape, F32),
        scratch_shapes=[pltpu.VMEM((D_MODEL, D_FF), BF16),
                        pltpu.VMEM((D_MODEL, D_FF), BF16),
                        pltpu.VMEM((D_FF, D_MODEL), BF16),
                        pltpu.VMEM((2, D_MODEL, FFN_CHUNK), F32),
                        pltpu.VMEM((2, FFN_CHUNK, D_MODEL), F32),
                        pltpu.SemaphoreType.DMA((2,)),
                        pltpu.SemaphoreType.DMA((2,))],
        compiler_params=_params(56, 1),
        name="ffn_mix" if mix is not None else "ffn",
    )(*operands)


K_SECTION = 2


def _inproj_kernel(x_ref, g_ref, w_ref, wkt_ref, *out_refs):
    h = (_rms_rows(x_ref[...]) * g_ref[...]).astype(BF16)
    off = 0
    for n, (o_ref, width) in enumerate(zip(out_refs, IN_SECTIONS)):
        if n == K_SECTION:
            o_ref[...] = lax.dot_general(wkt_ref[...], h, (((1,), (1,)), ((), ())),
                                         preferred_element_type=F32)
        else:
            o_ref[...] = _dot(h, w_ref[:, off:off + width])
        off += width


def _inproj(x, gain, w_in, w_k_t, layer):
    t = x.shape[0]
    row = lambda w: pl.BlockSpec((PROJ_ROWS, w), lambda i: (i, 0))
    out_specs = [row(w) for w in IN_SECTIONS]
    out_shape = [jax.ShapeDtypeStruct((t, w), F32) for w in IN_SECTIONS]
    out_specs[K_SECTION] = pl.BlockSpec((RET_WIDTH, PROJ_ROWS), lambda i: (0, i))
    out_shape[K_SECTION] = jax.ShapeDtypeStruct((RET_WIDTH, t), F32)
    return pl.pallas_call(
        _inproj_kernel,
        grid=(t // PROJ_ROWS,),
        in_specs=[row(D_MODEL),
                  _resident((1, D_MODEL), layer),
                  _resident((D_MODEL, IN_WIDTH), layer),
                  _resident((RET_WIDTH, D_MODEL), layer)],
        out_specs=out_specs,
        out_shape=out_shape,
        compiler_params=_params(40, 1),
        name="inproj",
    )(x, gain, w_in, w_k_t)


S5_CHUNK = 8
S5_CAT = S5_CHUNK * S5_WIDTH
S5_CONST_ROWS = 8 * SUBLANES


def _cmul(a, b):
    return a[0] * b[0] - a[1] * b[1], a[0] * b[1] + a[1] * b[0]


def _s5_prep_kernel(lr_ref, li_ref, ls_ref, er_ref, ei_ref, ctr_ref, cti_ref, cst_ref,
                    wst_ref, kk_ref, vt_ref, c_ref):
    k = pl.program_id(1)
    lr = lr_ref[...]
    li = li_ref[...]
    step = jnp.exp(ls_ref[...])
    ar = lr * step
    ai = li * step
    mag = jnp.exp(ar)
    p1 = (mag * jnp.cos(ai), mag * jnp.sin(ai))
    nr = p1[0] - 1.0
    den = lr * lr + li * li
    f = ((nr * lr + p1[1] * li) / den, (p1[1] * lr - nr * li) / den)
    powers = [(jnp.ones_like(lr), jnp.zeros_like(lr)), p1]
    for _ in range(2, S5_CHUNK + 1):
        powers.append(_cmul(powers[-1], p1))

    def pick(n):
        out = powers[0]
        for i in range(1, S5_CHUNK + 1):
            out = (jnp.where(n == i, powers[i][0], out[0]), jnp.where(n == i, powers[i][1], out[1]))
        return out

    fp = _cmul(f, pick(k))
    er = er_ref[...]
    ei = ei_ref[...]
    w = jnp.concatenate([fp[0] * er - fp[1] * ei, fp[0] * ei + fp[1] * er], axis=1)
    wb = w.astype(BF16)
    wst_ref[...] = wb
    kk_ref[...] = _dot(wb, cst_ref[...])
    a, b = pick(k + 1)
    ctr = ctr_ref[...]
    cti = cti_ref[...]
    vt_ref[:, :S5_CH] = (ctr * a - cti * b).astype(BF16)
    vt_ref[:, S5_CH:] = (-(ctr * b) - cti * a).astype(BF16)

    @pl.when(k == 0)
    def _():
        q = [powers[S5_CHUNK]]
        for _ in range(1, SUBLANES):
            q.append(_cmul(q[-1], q[0]))
        row = lax.broadcasted_iota(jnp.int32, (SUBLANES, S5_CH), 0)
        zero = jnp.zeros((SUBLANES, S5_CH), F32)
        for s, d in enumerate((1, 2, 4)):
            for part in range(2):
                base = (2 * s + part) * SUBLANES
                c_ref[base:base + SUBLANES, :] = jnp.where(
                    row >= d, jnp.broadcast_to(q[d - 1][part], (SUBLANES, S5_CH)), zero)
        for part in range(2):
            tile = zero
            for r in range(SUBLANES):
                tile = jnp.where(row == r, jnp.broadcast_to(q[r][part], (SUBLANES, S5_CH)), tile)
            base = (6 + part) * SUBLANES
            c_ref[base:base + SUBLANES, :] = tile


def _s5_prepare(lam_re, lam_im, log_step, b_re, b_im, c_re, c_im):
    depth = lam_re.shape[0]
    def embed(a):
        n, m = a.shape[2:]
        same = np.equal.outer(np.repeat(np.arange(S5_GROUPS), n), np.arange(S5_GROUPS))
        tiled = jnp.broadcast_to(a.reshape(depth, S5_GROUPS * n, 1, m),
                                 (depth, S5_GROUPS * n, S5_GROUPS, m))
        return jnp.where(same[None, :, :, None], tiled, 0.0).reshape(
            depth, S5_GROUPS * n, S5_GROUPS * m)

    b_rows = lambda b: embed(jnp.swapaxes(b, 2, 3))
    c_rows = embed
    c_cols = lambda c: embed(jnp.swapaxes(c, 2, 3))
    c_stack = jnp.concatenate([c_cols(c_re), -c_cols(c_im)], axis=1).astype(BF16)
    vec = lambda a: a.reshape(depth, 1, S5_CH)
    ls = jnp.broadcast_to(log_step[:, :, None], (depth, S5_GROUPS, S5_STATE))
    vspec = pl.BlockSpec((None, 1, S5_CH), lambda l, k: (l, 0, 0))
    mspec = pl.BlockSpec((None, S5_WIDTH, S5_CH), lambda l, k: (l, 0, 0))
    wst, kk, vt, consts = pl.pallas_call(
        _s5_prep_kernel,
        grid=(depth, S5_CHUNK),
        in_specs=[vspec, vspec, vspec, mspec, mspec, mspec, mspec,
                  pl.BlockSpec((None, 2 * S5_CH, S5_WIDTH), lambda l, k: (l, 0, 0))],
        out_specs=[pl.BlockSpec((None, S5_WIDTH, 2 * S5_CH),
                                lambda l, k: (l, S5_CHUNK - 1 - k, 0)),
                   pl.BlockSpec((None, None, S5_WIDTH, S5_WIDTH), lambda l, k: (l, k, 0, 0)),
                   pl.BlockSpec((None, S5_WIDTH, 2 * S5_CH), lambda l, k: (l, k, 0)),
                   pl.BlockSpec((None, S5_CONST_ROWS, S5_CH), lambda l, k: (l, 0, 0))],
        out_shape=[jax.ShapeDtypeStruct((depth, S5_CAT, 2 * S5_CH), BF16),
                   jax.ShapeDtypeStruct((depth, S5_CHUNK, S5_WIDTH, S5_WIDTH), F32),
                   jax.ShapeDtypeStruct((depth, S5_CAT, 2 * S5_CH), BF16),
                   jax.ShapeDtypeStruct((depth, S5_CONST_ROWS, S5_CH), F32)],
        compiler_params=_params(40, 2),
        name="s5_prep",
    )(vec(lam_re), vec(lam_im), vec(ls), b_rows(b_re), b_rows(b_im), c_rows(c_re), c_rows(c_im),
      c_stack)
    zero = jnp.zeros((depth, S5_WIDTH, S5_WIDTH), F32)
    m = jnp.concatenate(
        [jnp.concatenate([zero] * j + [kk[:, k] for k in range(S5_CHUNK - j)], axis=-1)
         for j in range(S5_CHUNK)], axis=-2).astype(BF16)
    return wst, m, vt, consts


def _s5_state_kernel(ulo_ref, uhi_ref, wst_ref, k_ref, uc_ref, sp_ref, ds_scr, s_scr):
    j = pl.program_id(1)
    nch = ulo_ref.shape[0] // S5_CHUNK
    ncol = S5_CH // LANES
    cols = [(slice(cb * LANES, (cb + 1) * LANES),
             slice(S5_CH + cb * LANES, S5_CH + (cb + 1) * LANES)) for cb in range(ncol)]

    @pl.when(j == 0)
    def _():
        s_scr[0:SUBLANES, :] = jnp.zeros((SUBLANES, 2 * S5_CH), F32)

    @pl.when(j > 0)
    def _():
        s_scr[0:SUBLANES, :] = s_scr[nch:nch + SUBLANES, :]

    uc = jnp.concatenate([half[pl.ds(jj, nch, stride=S5_CHUNK), :]
                          for jj in range(S5_CHUNK) for half in (ulo_ref, uhi_ref)],
                         axis=1).astype(BF16)
    uc_ref[...] = uc
    ds_scr[...] = _dot(uc, wst_ref[...])

    def tile_body(i, carry):
        off = pl.multiple_of(i * SUBLANES, SUBLANES)
        loaded = [(ds_scr[pl.ds(off, SUBLANES), re], ds_scr[pl.ds(off, SUBLANES), im])
                  for re, im in cols]
        done = []
        for cb, (br, bi) in enumerate(loaded):
            re = cols[cb][0]
            for s, d in enumerate((1, 2, 4)):
                cr = k_ref[(2 * s) * SUBLANES:(2 * s + 1) * SUBLANES, re]
                ci = k_ref[(2 * s + 1) * SUBLANES:(2 * s + 2) * SUBLANES, re]
                rr = pltpu.roll(br, d, 0)
                ri = pltpu.roll(bi, d, 0)
                br, bi = br + (cr * rr - ci * ri), bi + (cr * ri + ci * rr)
            pr, pi = carry[cb], carry[ncol + cb]
            wr = k_ref[6 * SUBLANES:7 * SUBLANES, re]
            wi = k_ref[7 * SUBLANES:8 * SUBLANES, re]
            done.append((br + (wr * pr - wi * pi), bi + (wr * pi + wi * pr)))
        for (re, im), (sr, si) in zip(cols, done):
            s_scr[pl.ds(off + SUBLANES, SUBLANES), re] = sr
            s_scr[pl.ds(off + SUBLANES, SUBLANES), im] = si
        return tuple(last(sr) for sr, _ in done) + tuple(last(si) for _, si in done)

    last = lambda a: jnp.broadcast_to(a[SUBLANES - 1:SUBLANES, :], (SUBLANES, LANES))
    init = (tuple(last(s_scr[0:SUBLANES, re]) for re, _ in cols)
            + tuple(last(s_scr[0:SUBLANES, im]) for _, im in cols))
    lax.fori_loop(0, nch // SUBLANES, tile_body, init)

    sp_ref[...] = s_scr[SUBLANES - 1:SUBLANES - 1 + nch, :].astype(BF16)


def _s5_state(u, wst, consts, layer, batch):
    t = u.shape[0]
    nblk = t // batch // S5_STATE_ROWS
    nch = S5_STATE_ROWS // S5_CHUNK
    chunk_rows = pl.BlockSpec((nch, S5_CAT), lambda b, j: (b * nblk + j, 0))
    return pl.pallas_call(
        _s5_state_kernel,
        grid=(batch, nblk),
        in_specs=[pl.BlockSpec((S5_STATE_ROWS, LANES), lambda b, j: (b * nblk + j, 0)),
                  pl.BlockSpec((S5_STATE_ROWS, LANES), lambda b, j: (b * nblk + j, 1)),
                  _resident((S5_CAT, 2 * S5_CH), layer),
                  _resident((S5_CONST_ROWS, S5_CH), layer)],
        out_specs=[chunk_rows, chunk_rows],
        out_shape=[jax.ShapeDtypeStruct((t // S5_CHUNK, S5_CAT), BF16),
                   jax.ShapeDtypeStruct((t // S5_CHUNK, 2 * S5_CH), BF16)],
        scratch_shapes=[pltpu.VMEM((nch, 2 * S5_CH), F32),
                        pltpu.VMEM((nch + SUBLANES, 2 * S5_CH), F32)],
        compiler_params=_params(44, 2),
        name="s5_state",
    )(u, u, wst, consts)


def _s5_out_kernel(u_ref, uc_ref, sp_ref, m_ref, vt_ref, d_ref, wglu_ref, bglu_ref, gain_ref,
                   o_ref, ylo_scr, yhi_scr):
    nch = uc_ref.shape[0]
    sp = sp_ref[...]
    for r in range(S5_CHUNK):
        cols = slice(r * S5_WIDTH, (r + 1) * S5_WIDTH)
        used = (r + 1) * S5_WIDTH
        y_r = _dot(uc_ref[:, :used], m_ref[:used, cols])
        y_r = y_r + lax.dot_general(sp, vt_ref[cols, :], (((1,), (1,)), ((), ())),
                                    preferred_element_type=F32)
        ylo_scr[pl.ds(r, nch, stride=S5_CHUNK), :] = y_r[:, :LANES]
        yhi_scr[pl.ds(r, nch, stride=S5_CHUNK), :] = y_r[:, LANES:]
    y = jnp.concatenate([ylo_scr[...], yhi_scr[...]], axis=1) + d_ref[...] * u_ref[...]
    y = _gelu_tanh(y)
    y = y * _sigmoid(_dot(y.astype(BF16), wglu_ref[...]) + bglu_ref[...])
    o_ref[...] = (_rms_rows(y) * gain_ref[...]).astype(BF16)


def _s5_out(u, uc, sp, m, vt, d_skip, w_glu, b_glu, gain, layer):
    t = u.shape[0]
    nch = S5_OUT_ROWS // S5_CHUNK
    row = pl.BlockSpec((S5_OUT_ROWS, S5_WIDTH), lambda i: (i, 0))
    chunk_rows = pl.BlockSpec((nch, S5_CAT), lambda i: (i, 0))
    return pl.pallas_call(
        _s5_out_kernel,
        grid=(t // S5_OUT_ROWS,),
        in_specs=[row, chunk_rows, chunk_rows,
                  _resident((S5_CAT, S5_CAT), layer),
                  _resident((S5_CAT, 2 * S5_CH), layer),
                  _resident((1, S5_WIDTH), layer),
                  _resident((S5_WIDTH, S5_WIDTH), layer),
                  _resident((1, S5_WIDTH), layer),
                  _resident((1, S5_WIDTH), layer)],
        out_specs=row,
        out_shape=jax.ShapeDtypeStruct((t, S5_WIDTH), BF16),
        scratch_shapes=[pltpu.VMEM((S5_OUT_ROWS, LANES), F32),
                        pltpu.VMEM((S5_OUT_ROWS, LANES), F32)],
        compiler_params=_params(48, 1),
        name="s5_out",
    )(u, uc, sp, m, vt, d_skip, w_glu, b_glu, gain)


def _lru_kernel(x_ref, gate_ref, cw_ref, cb_ref, wax_ref, bax_ref, lam_ref, gain_ref, o_ref,
                ext_scr, a_scr, b_scr, h_scr, carry_scr):
    j = pl.program_id(1)
    rows = x_ref.shape[0]
    pad = SUBLANES

    @pl.when(j == 0)
    def _():
        ext_scr[0:pad, :] = jnp.zeros((pad, LRU_WIDTH), F32)
        carry_scr[...] = jnp.zeros(carry_scr.shape, F32)

    @pl.when(j > 0)
    def _():
        ext_scr[0:pad, :] = ext_scr[rows:rows + pad, :]

    ext_scr[pad:rows + pad, :] = x_ref[...]
    xc = cb_ref[...] + cw_ref[CONV_WIDTH - 1:CONV_WIDTH, :] * ext_scr[pad:rows + pad, :]
    for lag in range(1, CONV_WIDTH):
        tap = cw_ref[CONV_WIDTH - 1 - lag:CONV_WIDTH - lag, :]
        xc = xc + tap * ext_scr[pad - lag:rows + pad - lag, :]

    gates = _sigmoid(_dot(xc.astype(BF16), wax_ref[...]) + bax_ref[...])
    r = gates[:, :LRU_WIDTH]
    i = gates[:, LRU_WIDTH:]
    z = -lam_ref[...]
    softplus = jnp.maximum(z, 0.0) + jnp.log1p(jnp.exp(-jnp.abs(z)))
    a = jnp.exp((-LRU_C) * r * softplus)
    a_scr[...] = a
    b_scr[...] = jnp.sqrt(1.0 - a * a) * (i * xc)

    row = lax.broadcasted_iota(jnp.int32, (SUBLANES, LRU_WIDTH), 0)

    def tile_body(t, prev):
        off = pl.multiple_of(t * SUBLANES, SUBLANES)
        at = a_scr[pl.ds(off, SUBLANES), :]
        bt = b_scr[pl.ds(off, SUBLANES), :]
        for d in (1, 2, 4):
            keep = row >= d
            ar = jnp.where(keep, pltpu.roll(at, d, 0), 1.0)
            br = jnp.where(keep, pltpu.roll(bt, d, 0), 0.0)
            bt = at * br + bt
            at = at * ar
        h = at * prev + bt
        h_scr[pl.ds(off, SUBLANES), :] = h
        return jnp.broadcast_to(h[SUBLANES - 1:SUBLANES, :], (SUBLANES, LRU_WIDTH))

    carry_scr[...] = lax.fori_loop(0, rows // SUBLANES, tile_body, carry_scr[...],
                                   unroll=LRU_UNROLL)

    y = h_scr[...] * _gelu_tanh(gate_ref[...])
    o_ref[...] = (_rms_rows(y) * gain_ref[...]).astype(BF16)


def _lru(x, gate, conv_w, conv_b, wax, bax, lam, gain, layer, batch):
    t = x.shape[0]
    nblk = t // batch // LRU_ROWS
    row = pl.BlockSpec((LRU_ROWS, LRU_WIDTH), lambda b, j: (b * nblk + j, 0))
    return pl.pallas_call(
        _lru_kernel,
        grid=(batch, nblk),
        in_specs=[row, row,
                  _resident((CONV_WIDTH, LRU_WIDTH), layer),
                  _resident((1, LRU_WIDTH), layer),
                  _resident((LRU_WIDTH, 2 * LRU_WIDTH), layer),
                  _resident((1, 2 * LRU_WIDTH), layer),
                  _resident((1, LRU_WIDTH), layer),
                  _resident((1, LRU_WIDTH), layer)],
        out_specs=row,
        out_shape=jax.ShapeDtypeStruct((t, LRU_WIDTH), BF16),
        scratch_shapes=[pltpu.VMEM((LRU_ROWS + SUBLANES, LRU_WIDTH), F32),
                        pltpu.VMEM((LRU_ROWS, LRU_WIDTH), F32),
                        pltpu.VMEM((LRU_ROWS, LRU_WIDTH), F32),
                        pltpu.VMEM((LRU_ROWS, LRU_WIDTH), F32),
                        pltpu.VMEM((SUBLANES, LRU_WIDTH), F32)],
        compiler_params=_params(32, 2),
        name="lru",
    )(x, gate, conv_w, conv_b, wax, bax, lam, gain)


def _ret_kernel(q_ref, kt_ref, v_ref, g_ref, cos_ref, sin_ref, cost_ref, sint_ref, decay_ref,
                zetat_ref, xi_ref, gamma_ref, gain_ref, o_ref, state_scr, y_scr):
    j = pl.program_id(1)

    @pl.when(j == 0)
    def _():
        state_scr[...] = jnp.zeros(state_scr.shape, F32)

    half = RET_HEAD_DIM // 2

    def chunk(start):
        rows = pl.ds(start, RET_CHUNK)
        cs = cos_ref[rows, :]
        sn = sin_ref[rows, :]
        cst = cost_ref[:, rows]
        snt = sint_ref[:, rows]
        ssq = jnp.zeros((RET_CHUNK, 1), F32)
        for h in range(RET_HEADS):
            cols = slice(h * RET_HEAD_DIM, (h + 1) * RET_HEAD_DIM)
            qh = q_ref[rows, cols]
            kt = kt_ref[cols, rows]
            vh = v_ref[rows, cols].astype(BF16)
            qb = (qh * cs + pltpu.roll(qh, half, 1) * sn).astype(BF16)
            kr = kt * cst + jnp.concatenate([kt[half:], kt[:half]], axis=0) * snt
            scores = _dot(qb, kr.astype(BF16)) * decay_ref[:, cols]
            inner = _dot(scores.astype(BF16), vh)
            st = state_scr[:, cols]
            cross = _dot(qb, st.astype(BF16)) * xi_ref[:, cols]
            kz = (kr * zetat_ref[cols, :]).astype(BF16)
            state_scr[:, cols] = gamma_ref[:, cols] * st + _dot(kz, vh)
            yh = _silu(g_ref[rows, cols]) * _rms_rows(inner + cross)
            ssq = ssq + jnp.sum(yh * yh, axis=-1, keepdims=True)
            y_scr[:, cols] = yh
        inv = lax.rsqrt(ssq * (1.0 / RET_WIDTH) + NORM_EPS)
        o_ref[rows, :] = ((y_scr[...] * inv) * gain_ref[...]).astype(BF16)

    def body(i, carry):
        for c in range(RET_UNROLL):
            chunk(pl.multiple_of((i * RET_UNROLL + c) * RET_CHUNK, RET_CHUNK))
        return carry

    lax.fori_loop(0, q_ref.shape[0] // (RET_CHUNK * RET_UNROLL), body, 0)


def _ret_tables(seq):
    dh, c, nh = RET_HEAD_DIM, RET_CHUNK, RET_HEADS
    pos = np.arange(seq, dtype=np.float64)
    inv_freq = ROPE_BASE ** (-np.arange(0, dh, 2, dtype=np.float64) / dh)
    ang = pos[:, None] * inv_freq[None, :]
    cos = np.cos(ang)
    sin = np.sin(ang)
    cos2 = np.concatenate([cos, cos], axis=-1)
    sin2 = np.concatenate([-sin, sin], axis=-1)
    log_gamma = np.log1p(-np.exp2(-5.0 - np.arange(nh, dtype=np.float64)))
    idx = np.arange(c, dtype=np.float64)
    diff = idx[:, None] - idx[None, :]
    decay = np.where(diff[None] >= 0,
                     np.exp(np.maximum(diff, 0.0)[None] * log_gamma[:, None, None]), 0.0)
    zeta = np.exp((c - 1.0 - idx)[None] * log_gamma[:, None])
    xi = np.exp((idx + 1.0)[None] * log_gamma[:, None])
    gamma_chunk = np.exp(c * log_gamma)
    decay_l = decay.transpose(1, 0, 2).reshape(c, nh * c)
    zeta_t = np.broadcast_to(zeta[:, None, :], (nh, dh, c)).reshape(nh * dh, c)
    xi_l = np.broadcast_to(xi.T[:, :, None], (c, nh, dh)).reshape(c, nh * dh)
    gamma_l = np.broadcast_to(gamma_chunk[:, None], (nh, dh)).reshape(1, nh * dh)
    scale = dh ** -0.5
    tables = (cos2 * scale, sin2 * scale, cos2.T, sin2.T, decay_l, zeta_t, xi_l, gamma_l)
    return tuple(jnp.asarray(np.ascontiguousarray(a, dtype=np.float32)) for a in tables)


def _ret(q, k_t, v, g, tables, gain, layer, batch):
    t = q.shape[0]
    nblk = t // batch // RET_ROWS
    row = pl.BlockSpec((RET_ROWS, RET_WIDTH), lambda b, j: (b * nblk + j, 0))
    row_t = pl.BlockSpec((RET_WIDTH, RET_ROWS), lambda b, j: (0, b * nblk + j))
    rot = pl.BlockSpec((RET_ROWS, RET_HEAD_DIM), lambda b, j: (j, 0))
    rot_t = pl.BlockSpec((RET_HEAD_DIM, RET_ROWS), lambda b, j: (0, j))
    const = lambda shape: pl.BlockSpec(shape, lambda b, j: (0, 0), pipeline_mode=pl.Buffered(1))
    return pl.pallas_call(
        _ret_kernel,
        grid=(batch, nblk),
        in_specs=[row, row_t, row, row, rot, rot, rot_t, rot_t,
                  const((RET_CHUNK, RET_WIDTH)), const((RET_WIDTH, RET_CHUNK)),
                  const((RET_CHUNK, RET_WIDTH)), const((1, RET_WIDTH)),
                  _resident((1, RET_WIDTH), layer)],
        out_specs=row,
        out_shape=jax.ShapeDtypeStruct((t, RET_WIDTH), BF16),
        scratch_shapes=[pltpu.VMEM((RET_HEAD_DIM, RET_WIDTH), F32),
                        pltpu.VMEM((RET_CHUNK, RET_WIDTH), F32)],
        compiler_params=_params(32, 2),
        name="ret",
    )(q, k_t, v, g, *tables, gain)


def _block_diag(w):
    depth, n, d, e = w.shape
    eye = jnp.eye(n, dtype=w.dtype)
    return jnp.einsum('lnde,nm->lndme', w, eye).reshape(depth, n * d, n * e)


def kernel(x, ffn1_norm, ffn1_w_gate, ffn1_w_up, ffn1_w_down, mix_norm, w_in, s5_lambda_re, s5_lambda_im, s5_log_step, s5_b_re, s5_b_im, s5_c_re, s5_c_im, s5_d, s5_w_glu, s5_b_glu, s5_out_norm, ret_out_norm, lru_conv_w, lru_conv_b, lru_w_a, lru_b_a, lru_w_x, lru_b_x, lru_lambda, lru_out_norm, w_out, ffn2_norm, ffn2_w_gate, ffn2_w_up, ffn2_w_down, final_norm):
    batch, seq, d = x.shape
    depth = w_in.shape[0]
    t = batch * seq
    assert d == D_MODEL and seq % max(S5_STATE_ROWS, LRU_ROWS, RET_ROWS) == 0
    assert t % max(FFN_ROWS, PROJ_ROWS, S5_OUT_ROWS) == 0

    row3 = lambda a: a.reshape(depth, 1, a.shape[-1])
    bf = lambda a: a.astype(BF16)

    f1 = (row3(ffn1_norm), ffn1_w_gate, ffn1_w_up, ffn1_w_down)
    f2 = (row3(ffn2_norm), ffn2_w_gate, ffn2_w_up, ffn2_w_down)
    w_in_b = bf(w_in)
    k_off = sum(IN_SECTIONS[:K_SECTION])
    w_k = lax.optimization_barrier(w_in[:, :, k_off:k_off + RET_WIDTH])
    w_k_t = bf(jnp.swapaxes(w_k, 1, 2))
    w_out_b = bf(w_out)
    s5_wst, s5_m, s5_vt, s5_consts = _s5_prepare(s5_lambda_re, s5_lambda_im, s5_log_step,
                                                 s5_b_re, s5_b_im, s5_c_re, s5_c_im)
    s5_glu = bf(s5_w_glu)
    lru_wax = bf(jnp.concatenate([_block_diag(lru_w_a), _block_diag(lru_w_x)], axis=-1))
    lru_bax = row3(jnp.concatenate([lru_b_a, lru_b_x], axis=-1))
    tables = _ret_tables(seq)

    xt = x.reshape(t, d)
    for l in range(depth):
        xt = _ffn(xt, *f1, l)
        u, q, k_t, v, g, xl, gl = _inproj(xt, row3(mix_norm), w_in_b, w_k_t, l)
        uc, sp = _s5_state(u, s5_wst, s5_consts, l, batch)
        y_s5 = _s5_out(u, uc, sp, s5_m, s5_vt, row3(s5_d), s5_glu, row3(s5_b_glu),
                       row3(s5_out_norm), l)
        y_ret = _ret(q, k_t, v, g, tables, row3(ret_out_norm), l, batch)
        y_lru = _lru(xl, gl, lru_conv_w, row3(lru_conv_b), lru_wax, lru_bax, row3(lru_lambda),
                     row3(lru_out_norm), l, batch)
        last = final_norm.reshape(1, d) if l == depth - 1 else None
        xt = _ffn(xt, *f2, l, mix=(y_s5, y_ret, y_lru, w_out_b), final_gain=last)
    return xt.reshape(batch, seq, d)
```

```python
import functools
import math

import jax
import jax.numpy as jnp
import numpy as np
from jax import lax
from jax.experimental import pallas as pl
from jax.experimental.pallas import tpu as pltpu

F32 = jnp.float32
BF16 = jnp.bfloat16

D_MODEL = 1024
D_FF = 2816
S5_WIDTH = 256
S5_GROUP_DIM = 16
S5_GROUPS = 16
S5_STATE = 64
S5_CH = S5_GROUPS * S5_STATE
RET_WIDTH = 512
RET_HEAD_DIM = 128
RET_HEADS = 4
RET_CHUNK = 128
LRU_WIDTH = 256
LRU_BLOCKS = 4
LRU_BLOCK_DIM = 64
CONV_WIDTH = 4
LRU_C = 8.0
ROPE_BASE = 10000.0
NORM_EPS = 1e-6
IN_SECTIONS = (S5_WIDTH, RET_WIDTH, RET_WIDTH, RET_WIDTH, RET_WIDTH, LRU_WIDTH, LRU_WIDTH)
IN_WIDTH = sum(IN_SECTIONS)

SUBLANES = 8
LANES = 128
MIB = 1024 * 1024

FFN_ROWS = 512
FFN_CHUNK = 256
PROJ_ROWS = 512
S5_STATE_ROWS = 4096
S5_OUT_ROWS = 2048
LRU_ROWS = 1024
LRU_UNROLL = 4
RET_ROWS = 1024
RET_UNROLL = 2


def _params(vmem_mib, n_axes):
    return pltpu.CompilerParams(
        dimension_semantics=("arbitrary",) * n_axes,
        vmem_limit_bytes=vmem_mib * MIB)


def _rms_rows(x):
    return x * lax.rsqrt(jnp.mean(x * x, axis=-1, keepdims=True) + NORM_EPS)


def _gelu_tanh(x):
    c = math.sqrt(2.0 / math.pi)
    return x * (0.5 * (1.0 + jnp.tanh(c * (x + 0.044715 * (x * x * x)))))


def _sigmoid(x):
    return 0.5 + 0.5 * jnp.tanh(0.5 * x)


def _silu(x):
    h = 0.5 * x
    return h + h * jnp.tanh(h)


def _dot(a, b):
    return jnp.dot(a, b, preferred_element_type=F32)


def _resident(shape, layer):
    nd = len(shape)
    return pl.BlockSpec((None,) + tuple(shape), lambda *_: (layer,) + (0,) * nd,
                        pipeline_mode=pl.Buffered(1))


def _ffn_kernel(*refs, layer, with_outproj, with_final_norm):
    refs = list(refs)
    wg_s, wu_s, wd_s, stage_g, stage_u, stage_d, sem = refs[-7:]
    o_ref = refs[-8]
    refs = refs[:-8]
    x_ref = refs.pop(0)
    if with_outproj:
        ys_ref, yr_ref, yl_ref, wo_ref = refs[:4]
        refs = refs[4:]
    g_ref, wg_hbm, wu_hbm, wd_hbm = refs[:4]

    x = x_ref[...]
    if with_outproj:
        x = x + _dot(ys_ref[...], wo_ref[0:S5_WIDTH, :])
        x = x + _dot(yr_ref[...], wo_ref[S5_WIDTH:S5_WIDTH + RET_WIDTH, :])
        x = x + _dot(yl_ref[...], wo_ref[S5_WIDTH + RET_WIDTH:, :])
    h = (_rms_rows(x) * g_ref[...]).astype(BF16)

    nchunk = D_FF // FFN_CHUNK
    chunks = [slice(c * FFN_CHUNK, (c + 1) * FFN_CHUNK) for c in range(nchunk)]

    def copies(c):
        sl, slot = chunks[c], c % 2
        return (pltpu.make_async_copy(wg_hbm.at[layer, :, sl], stage_g.at[slot], sem.at[0, slot]),
                pltpu.make_async_copy(wu_hbm.at[layer, :, sl], stage_u.at[slot], sem.at[1, slot]),
                pltpu.make_async_copy(wd_hbm.at[layer, sl, :], stage_d.at[slot], sem.at[2, slot]))

    first = pl.program_id(0) == 0

    @pl.when(first)
    def _():
        for c in range(min(2, nchunk)):
            for cp in copies(c):
                cp.start()

    acc = jnp.zeros(x.shape, F32)
    for c, sl in enumerate(chunks):
        @pl.when(first)
        def _(c=c, sl=sl):
            for cp in copies(c):
                cp.wait()
            wg_s[:, sl] = stage_g[c % 2].astype(BF16)
            wu_s[:, sl] = stage_u[c % 2].astype(BF16)
            wd_s[sl, :] = stage_d[c % 2].astype(BF16)
            if c + 2 < nchunk:
                for cp in copies(c + 2):
                    cp.start()

        act = (_silu(_dot(h, wg_s[:, sl])) * _dot(h, wu_s[:, sl])).astype(BF16)
        acc = acc + _dot(act, wd_s[sl, :])
    y = x + 0.5 * acc
    if with_final_norm:
        y = _rms_rows(y) * refs[4][...]
    o_ref[...] = y


def _ffn(x, gain, wg, wu, wd, layer, mix=None, final_gain=None):
    t = x.shape[0]
    rows = FFN_ROWS
    spec = lambda w: pl.BlockSpec((rows, w), lambda i: (i, 0))
    hbm = pl.BlockSpec(memory_space=pl.ANY)
    operands = [x]
    in_specs = [spec(D_MODEL)]
    if mix is not None:
        y_s5, y_ret, y_lru, w_out = mix
        operands += [y_s5, y_ret, y_lru, w_out]
        in_specs += [spec(S5_WIDTH), spec(RET_WIDTH), spec(LRU_WIDTH),
                     _resident((D_MODEL, D_MODEL), layer)]
    operands += [gain, wg, wu, wd]
    in_specs += [_resident((1, D_MODEL), layer), hbm, hbm, hbm]
    if final_gain is not None:
        operands.append(final_gain)
        in_specs.append(pl.BlockSpec((1, D_MODEL), lambda i: (0, 0)))
    return pl.pallas_call(
        functools.partial(_ffn_kernel, layer=layer, with_outproj=mix is not None,
                          with_final_norm=final_gain is not None),
        grid=(t // rows,),
        in_specs=in_specs,
        out_specs=spec(D_MODEL),
        out_shape=jax.ShapeDtypeStruct(x.shape, F32),
        scratch_shapes=[pltpu.VMEM((D_MODEL, D_FF), BF16),
                        pltpu.VMEM((D_MODEL, D_FF), BF16),
                        pltpu.VMEM((D_FF, D_MODEL), BF16),
                        pltpu.VMEM((2, D_MODEL, FFN_CHUNK), F32),
                        pltpu.VMEM((2, D_MODEL, FFN_CHUNK), F32),
                        pltpu.VMEM((2, FFN_CHUNK, D_MODEL), F32),
                        pltpu.SemaphoreType.DMA((3, 2))],
        compiler_params=_params(58, 1),
        name="ffn_mix" if mix is not None else "ffn",
    )(*operands)


K_SECTION = 2


def _inproj_kernel(x_ref, g_ref, w_ref, wkt_ref, *out_refs):
    h = (_rms_rows(x_ref[...]) * g_ref[...]).astype(BF16)
    off = 0
    for n, (o_ref, width) in enumerate(zip(out_refs, IN_SECTIONS)):
        if n == K_SECTION:
            o_ref[...] = lax.dot_general(wkt_ref[...], h, (((1,), (1,)), ((), ())),
                                         preferred_element_type=F32)
        else:
            o_ref[...] = _dot(h, w_ref[:, off:off + width])
        off += width


def _inproj(x, gain, w_in, w_k_t, layer):
    t = x.shape[0]
    row = lambda w: pl.BlockSpec((PROJ_ROWS, w), lambda i: (i, 0))
    out_specs = [row(w) for w in IN_SECTIONS]
    out_shape = [jax.ShapeDtypeStruct((t, w), F32) for w in IN_SECTIONS]
    out_specs[K_SECTION] = pl.BlockSpec((RET_WIDTH, PROJ_ROWS), lambda i: (0, i))
    out_shape[K_SECTION] = jax.ShapeDtypeStruct((RET_WIDTH, t), F32)
    return pl.pallas_call(
        _inproj_kernel,
        grid=(t // PROJ_ROWS,),
        in_specs=[row(D_MODEL),
                  _resident((1, D_MODEL), layer),
                  _resident((D_MODEL, IN_WIDTH), layer),
                  _resident((RET_WIDTH, D_MODEL), layer)],
        out_specs=out_specs,
        out_shape=out_shape,
        compiler_params=_params(40, 1),
        name="inproj",
    )(x, gain, w_in, w_k_t)


S5_CHUNK = 8
S5_CAT = S5_CHUNK * S5_WIDTH
S5_CONST_ROWS = 8 * SUBLANES


def _cmul(a, b):
    return a[0] * b[0] - a[1] * b[1], a[0] * b[1] + a[1] * b[0]


def _s5_prep_kernel(lr_ref, li_ref, ls_ref, er_ref, ei_ref, ctr_ref, cti_ref, cst_ref,
                    wst_ref, kk_ref, vt_ref, c_ref):
    k = pl.program_id(1)
    lr = lr_ref[...]
    li = li_ref[...]
    step = jnp.exp(ls_ref[...])
    ar = lr * step
    ai = li * step
    mag = jnp.exp(ar)
    p1 = (mag * jnp.cos(ai), mag * jnp.sin(ai))
    nr = p1[0] - 1.0
    den = lr * lr + li * li
    f = ((nr * lr + p1[1] * li) / den, (p1[1] * lr - nr * li) / den)
    powers = [(jnp.ones_like(lr), jnp.zeros_like(lr)), p1]
    for _ in range(2, S5_CHUNK + 1):
        powers.append(_cmul(powers[-1], p1))

    def pick(n):
        out = powers[0]
        for i in range(1, S5_CHUNK + 1):
            out = (jnp.where(n == i, powers[i][0], out[0]), jnp.where(n == i, powers[i][1], out[1]))
        return out

    fp = _cmul(f, pick(k))
    er = er_ref[...]
    ei = ei_ref[...]
    w = jnp.concatenate([fp[0] * er - fp[1] * ei, fp[0] * ei + fp[1] * er], axis=1)
    wb = w.astype(BF16)
    wst_ref[...] = wb
    kk_ref[...] = _dot(wb, cst_ref[...])
    a, b = pick(k + 1)
    ctr = ctr_ref[...]
    cti = cti_ref[...]
    vt_ref[:, :S5_CH] = (ctr * a - cti * b).astype(BF16)
    vt_ref[:, S5_CH:] = (-(ctr * b) - cti * a).astype(BF16)

    @pl.when(k == 0)
    def _():
        q = [powers[S5_CHUNK]]
        for _ in range(1, SUBLANES):
            q.append(_cmul(q[-1], q[0]))
        row = lax.broadcasted_iota(jnp.int32, (SUBLANES, S5_CH), 0)
        zero = jnp.zeros((SUBLANES, S5_CH), F32)
        for s, d in enumerate((1, 2, 4)):
            for part in range(2):
                base = (2 * s + part) * SUBLANES
                c_ref[base:base + SUBLANES, :] = jnp.where(
                    row >= d, jnp.broadcast_to(q[d - 1][part], (SUBLANES, S5_CH)), zero)
        for part in range(2):
            tile = zero
            for r in range(SUBLANES):
                tile = jnp.where(row == r, jnp.broadcast_to(q[r][part], (SUBLANES, S5_CH)), tile)
            base = (6 + part) * SUBLANES
            c_ref[base:base + SUBLANES, :] = tile


def _s5_prepare(lam_re, lam_im, log_step, b_re, b_im, c_re, c_im):
    depth = lam_re.shape[0]
    def embed(a):
        n, m = a.shape[2:]
        same = np.equal.outer(np.repeat(np.arange(S5_GROUPS), n), np.arange(S5_GROUPS))
        tiled = jnp.broadcast_to(a.reshape(depth, S5_GROUPS * n, 1, m),
                                 (depth, S5_GROUPS * n, S5_GROUPS, m))
        return jnp.where(same[None, :, :, None], tiled, 0.0).reshape(
            depth, S5_GROUPS * n, S5_GROUPS * m)

    b_rows = lambda b: embed(jnp.swapaxes(b, 2, 3))
    c_rows = embed
    c_cols = lambda c: embed(jnp.swapaxes(c, 2, 3))
    c_stack = jnp.concatenate([c_cols(c_re), -c_cols(c_im)], axis=1).astype(BF16)
    vec = lambda a: a.reshape(depth, 1, S5_CH)
    ls = jnp.broadcast_to(log_step[:, :, None], (depth, S5_GROUPS, S5_STATE))
    vspec = pl.BlockSpec((None, 1, S5_CH), lambda l, k: (l, 0, 0))
    mspec = pl.BlockSpec((None, S5_WIDTH, S5_CH), lambda l, k: (l, 0, 0))
    wst, kk, vt, consts = pl.pallas_call(
        _s5_prep_kernel,
        grid=(depth, S5_CHUNK),
        in_specs=[vspec, vspec, vspec, mspec, mspec, mspec, mspec,
                  pl.BlockSpec((None, 2 * S5_CH, S5_WIDTH), lambda l, k: (l, 0, 0))],
        out_specs=[pl.BlockSpec((None, S5_WIDTH, 2 * S5_CH),
                                lambda l, k: (l, S5_CHUNK - 1 - k, 0)),
                   pl.BlockSpec((None, None, S5_WIDTH, S5_WIDTH), lambda l, k: (l, k, 0, 0)),
                   pl.BlockSpec((None, S5_WIDTH, 2 * S5_CH), lambda l, k: (l, k, 0)),
                   pl.BlockSpec((None, S5_CONST_ROWS, S5_CH), lambda l, k: (l, 0, 0))],
        out_shape=[jax.ShapeDtypeStruct((depth, S5_CAT, 2 * S5_CH), BF16),
                   jax.ShapeDtypeStruct((depth, S5_CHUNK, S5_WIDTH, S5_WIDTH), F32),
                   jax.ShapeDtypeStruct((depth, S5_CAT, 2 * S5_CH), BF16),
                   jax.ShapeDtypeStruct((depth, S5_CONST_ROWS, S5_CH), F32)],
        compiler_params=_params(40, 2),
        name="s5_prep",
    )(vec(lam_re), vec(lam_im), vec(ls), b_rows(b_re), b_rows(b_im), c_rows(c_re), c_rows(c_im),
      c_stack)
    zero = jnp.zeros((depth, S5_WIDTH, S5_WIDTH), F32)
    m = jnp.concatenate(
        [jnp.concatenate([zero] * j + [kk[:, k] for k in range(S5_CHUNK - j)], axis=-1)
         for j in range(S5_CHUNK)], axis=-2).astype(BF16)
    return wst, m, vt, consts


def _s5_state_kernel(ulo_ref, uhi_ref, wst_ref, k_ref, uc_ref, sp_ref, ds_scr, s_scr):
    j = pl.program_id(1)
    nch = ulo_ref.shape[0] // S5_CHUNK
    ncol = S5_CH // LANES
    cols = [(slice(cb * LANES, (cb + 1) * LANES),
             slice(S5_CH + cb * LANES, S5_CH + (cb + 1) * LANES)) for cb in range(ncol)]

    @pl.when(j == 0)
    def _():
        s_scr[0:SUBLANES, :] = jnp.zeros((SUBLANES, 2 * S5_CH), F32)

    @pl.when(j > 0)
    def _():
        s_scr[0:SUBLANES, :] = s_scr[nch:nch + SUBLANES, :]

    uc = jnp.concatenate([half[pl.ds(jj, nch, stride=S5_CHUNK), :]
                          for jj in range(S5_CHUNK) for half in (ulo_ref, uhi_ref)],
                         axis=1).astype(BF16)
    uc_ref[...] = uc
    ds_scr[...] = _dot(uc, wst_ref[...])

    def tile_body(i, carry):
        off = pl.multiple_of(i * SUBLANES, SUBLANES)
        loaded = [(ds_scr[pl.ds(off, SUBLANES), re], ds_scr[pl.ds(off, SUBLANES), im])
                  for re, im in cols]
        done = []
        for cb, (br, bi) in enumerate(loaded):
            re = cols[cb][0]
            for s, d in enumerate((1, 2, 4)):
                cr = k_ref[(2 * s) * SUBLANES:(2 * s + 1) * SUBLANES, re]
                ci = k_ref[(2 * s + 1) * SUBLANES:(2 * s + 2) * SUBLANES, re]
                rr = pltpu.roll(br, d, 0)
                ri = pltpu.roll(bi, d, 0)
                br, bi = br + (cr * rr - ci * ri), bi + (cr * ri + ci * rr)
            pr, pi = carry[cb], carry[ncol + cb]
            wr = k_ref[6 * SUBLANES:7 * SUBLANES, re]
            wi = k_ref[7 * SUBLANES:8 * SUBLANES, re]
            done.append((br + (wr * pr - wi * pi), bi + (wr * pi + wi * pr)))
        for (re, im), (sr, si) in zip(cols, done):
            s_scr[pl.ds(off + SUBLANES, SUBLANES), re] = sr
            s_scr[pl.ds(off + SUBLANES, SUBLANES), im] = si
        return tuple(last(sr) for sr, _ in done) + tuple(last(si) for _, si in done)

    last = lambda a: jnp.broadcast_to(a[SUBLANES - 1:SUBLANES, :], (SUBLANES, LANES))
    init = (tuple(last(s_scr[0:SUBLANES, re]) for re, _ in cols)
            + tuple(last(s_scr[0:SUBLANES, im]) for _, im in cols))
    lax.fori_loop(0, nch // SUBLANES, tile_body, init)

    sp_ref[...] = s_scr[SUBLANES - 1:SUBLANES - 1 + nch, :].astype(BF16)


def _s5_state(u, wst, consts, layer, batch):
    t = u.shape[0]
    nblk = t // batch // S5_STATE_ROWS
    nch = S5_STATE_ROWS // S5_CHUNK
    chunk_rows = pl.BlockSpec((nch, S5_CAT), lambda b, j: (b * nblk + j, 0))
    return pl.pallas_call(
        _s5_state_kernel,
        grid=(batch, nblk),
        in_specs=[pl.BlockSpec((S5_STATE_ROWS, LANES), lambda b, j: (b * nblk + j, 0)),
                  pl.BlockSpec((S5_STATE_ROWS, LANES), lambda b, j: (b * nblk + j, 1)),
                  _resident((S5_CAT, 2 * S5_CH), layer),
                  _resident((S5_CONST_ROWS, S5_CH), layer)],
        out_specs=[chunk_rows, chunk_rows],
        out_shape=[jax.ShapeDtypeStruct((t // S5_CHUNK, S5_CAT), BF16),
                   jax.ShapeDtypeStruct((t // S5_CHUNK, 2 * S5_CH), BF16)],
        scratch_shapes=[pltpu.VMEM((nch, 2 * S5_CH), F32),
                        pltpu.VMEM((nch + SUBLANES, 2 * S5_CH), F32)],
        compiler_params=_params(44, 2),
        name="s5_state",
    )(u, u, wst, consts)


def _s5_out_kernel(u_ref, uc_ref, sp_ref, m_ref, vt_ref, d_ref, wglu_ref, bglu_ref, gain_ref,
                   o_ref, ylo_scr, yhi_scr):
    nch = uc_ref.shape[0]
    sp = sp_ref[...]
    for r in range(S5_CHUNK):
        cols = slice(r * S5_WIDTH, (r + 1) * S5_WIDTH)
        used = (r + 1) * S5_WIDTH
        y_r = _dot(uc_ref[:, :used], m_ref[:used, cols])
        y_r = y_r + lax.dot_general(sp, vt_ref[cols, :], (((1,), (1,)), ((), ())),
                                    preferred_element_type=F32)
        ylo_scr[pl.ds(r, nch, stride=S5_CHUNK), :] = y_r[:, :LANES]
        yhi_scr[pl.ds(r, nch, stride=S5_CHUNK), :] = y_r[:, LANES:]
    y = jnp.concatenate([ylo_scr[...], yhi_scr[...]], axis=1) + d_ref[...] * u_ref[...]
    y = _gelu_tanh(y)
    y = y * _sigmoid(_dot(y.astype(BF16), wglu_ref[...]) + bglu_ref[...])
    o_ref[...] = (_rms_rows(y) * gain_ref[...]).astype(BF16)


def _s5_out(u, uc, sp, m, vt, d_skip, w_glu, b_glu, gain, layer):
    t = u.shape[0]
    nch = S5_OUT_ROWS // S5_CHUNK
    row = pl.BlockSpec((S5_OUT_ROWS, S5_WIDTH), lambda i: (i, 0))
    chunk_rows = pl.BlockSpec((nch, S5_CAT), lambda i: (i, 0))
    return pl.pallas_call(
        _s5_out_kernel,
        grid=(t // S5_OUT_ROWS,),
        in_specs=[row, chunk_rows, chunk_rows,
                  _resident((S5_CAT, S5_CAT), layer),
                  _resident((S5_CAT, 2 * S5_CH), layer),
                  _resident((1, S5_WIDTH), layer),
                  _resident((S5_WIDTH, S5_WIDTH), layer),
                  _resident((1, S5_WIDTH), layer),
                  _resident((1, S5_WIDTH), layer)],
        out_specs=row,
        out_shape=jax.ShapeDtypeStruct((t, S5_WIDTH), BF16),
        scratch_shapes=[pltpu.VMEM((S5_OUT_ROWS, LANES), F32),
                        pltpu.VMEM((S5_OUT_ROWS, LANES), F32)],
        compiler_params=_params(48, 1),
        name="s5_out",
    )(u, uc, sp, m, vt, d_skip, w_glu, b_glu, gain)


def _lru_kernel(x_ref, gate_ref, cw_ref, cb_ref, wax_ref, bax_ref, lam_ref, gain_ref, o_ref,
                ext_scr, a_scr, b_scr, h_scr, carry_scr):
    j = pl.program_id(1)
    rows = x_ref.shape[0]
    pad = SUBLANES

    @pl.when(j == 0)
    def _():
        ext_scr[0:pad, :] = jnp.zeros((pad, LRU_WIDTH), F32)
        carry_scr[...] = jnp.zeros(carry_scr.shape, F32)

    @pl.when(j > 0)
    def _():
        ext_scr[0:pad, :] = ext_scr[rows:rows + pad, :]

    ext_scr[pad:rows + pad, :] = x_ref[...]
    xc = cb_ref[...] + cw_ref[CONV_WIDTH - 1:CONV_WIDTH, :] * ext_scr[pad:rows + pad, :]
    for lag in range(1, CONV_WIDTH):
        tap = cw_ref[CONV_WIDTH - 1 - lag:CONV_WIDTH - lag, :]
        xc = xc + tap * ext_scr[pad - lag:rows + pad - lag, :]

    gates = _sigmoid(_dot(xc.astype(BF16), wax_ref[...]) + bax_ref[...])
    r = gates[:, :LRU_WIDTH]
    i = gates[:, LRU_WIDTH:]
    z = -lam_ref[...]
    softplus = jnp.maximum(z, 0.0) + jnp.log1p(jnp.exp(-jnp.abs(z)))
    a = jnp.exp((-LRU_C) * r * softplus)
    a_scr[...] = a
    b_scr[...] = jnp.sqrt(1.0 - a * a) * (i * xc)

    row = lax.broadcasted_iota(jnp.int32, (SUBLANES, LRU_WIDTH), 0)

    def tile_body(t, prev):
        off = pl.multiple_of(t * SUBLANES, SUBLANES)
        at = a_scr[pl.ds(off, SUBLANES), :]
        bt = b_scr[pl.ds(off, SUBLANES), :]
        for d in (1, 2, 4):
            keep = row >= d
            ar = jnp.where(keep, pltpu.roll(at, d, 0), 1.0)
            br = jnp.where(keep, pltpu.roll(bt, d, 0), 0.0)
            bt = at * br + bt
            at = at * ar
        h = at * prev + bt
        h_scr[pl.ds(off, SUBLANES), :] = h
        return jnp.broadcast_to(h[SUBLANES - 1:SUBLANES, :], (SUBLANES, LRU_WIDTH))

    carry_scr[...] = lax.fori_loop(0, rows // SUBLANES, tile_body, carry_scr[...],
                                   unroll=LRU_UNROLL)

    y = h_scr[...] * _gelu_tanh(gate_ref[...])
    o_ref[...] = (_rms_rows(y) * gain_ref[...]).astype(BF16)


def _lru(x, gate, conv_w, conv_b, wax, bax, lam, gain, layer, batch):
    t = x.shape[0]
    nblk = t // batch // LRU_ROWS
    row = pl.BlockSpec((LRU_ROWS, LRU_WIDTH), lambda b, j: (b * nblk + j, 0))
    return pl.pallas_call(
        _lru_kernel,
        grid=(batch, nblk),
        in_specs=[row, row,
                  _resident((CONV_WIDTH, LRU_WIDTH), layer),
                  _resident((1, LRU_WIDTH), layer),
                  _resident((LRU_WIDTH, 2 * LRU_WIDTH), layer),
                  _resident((1, 2 * LRU_WIDTH), layer),
                  _resident((1, LRU_WIDTH), layer),
                  _resident((1, LRU_WIDTH), layer)],
        out_specs=row,
        out_shape=jax.ShapeDtypeStruct((t, LRU_WIDTH), BF16),
        scratch_shapes=[pltpu.VMEM((LRU_ROWS + SUBLANES, LRU_WIDTH), F32),
                        pltpu.VMEM((LRU_ROWS, LRU_WIDTH), F32),
                        pltpu.VMEM((LRU_ROWS, LRU_WIDTH), F32),
                        pltpu.VMEM((LRU_ROWS, LRU_WIDTH), F32),
                        pltpu.VMEM((SUBLANES, LRU_WIDTH), F32)],
        compiler_params=_params(32, 2),
        name="lru",
    )(x, gate, conv_w, conv_b, wax, bax, lam, gain)


def _ret_kernel(q_ref, kt_ref, v_ref, g_ref, cos_ref, sin_ref, cost_ref, sint_ref, decay_ref,
                zetat_ref, xi_ref, gamma_ref, gain_ref, o_ref, state_scr, y_scr):
    j = pl.program_id(1)

    @pl.when(j == 0)
    def _():
        state_scr[...] = jnp.zeros(state_scr.shape, F32)

    half = RET_HEAD_DIM // 2

    def chunk(start):
        rows = pl.ds(start, RET_CHUNK)
        cs = cos_ref[rows, :]
        sn = sin_ref[rows, :]
        cst = cost_ref[:, rows]
        snt = sint_ref[:, rows]
        ssq = jnp.zeros((RET_CHUNK, 1), F32)
        for h in range(RET_HEADS):
            cols = slice(h * RET_HEAD_DIM, (h + 1) * RET_HEAD_DIM)
            qh = q_ref[rows, cols]
            kt = kt_ref[cols, rows]
            vh = v_ref[rows, cols].astype(BF16)
            qb = (qh * cs + pltpu.roll(qh, half, 1) * sn).astype(BF16)
            kr = kt * cst + jnp.concatenate([kt[half:], kt[:half]], axis=0) * snt
            scores = _dot(qb, kr.astype(BF16)) * decay_ref[:, cols]
            inner = _dot(scores.astype(BF16), vh)
            st = state_scr[:, cols]
            cross = _dot(qb, st.astype(BF16)) * xi_ref[:, cols]
            kz = (kr * zetat_ref[cols, :]).astype(BF16)
            state_scr[:, cols] = gamma_ref[:, cols] * st + _dot(kz, vh)
            yh = _silu(g_ref[rows, cols]) * _rms_rows(inner + cross)
            ssq = ssq + jnp.sum(yh * yh, axis=-1, keepdims=True)
            y_scr[:, cols] = yh
        inv = lax.rsqrt(ssq * (1.0 / RET_WIDTH) + NORM_EPS)
        o_ref[rows, :] = ((y_scr[...] * inv) * gain_ref[...]).astype(BF16)

    def body(i, carry):
        for c in range(RET_UNROLL):
            chunk(pl.multiple_of((i * RET_UNROLL + c) * RET_CHUNK, RET_CHUNK))
        return carry

    lax.fori_loop(0, q_ref.shape[0] // (RET_CHUNK * RET_UNROLL), body, 0)


def _ret_tables(seq):
    dh, c, nh = RET_HEAD_DIM, RET_CHUNK, RET_HEADS
    pos = np.arange(seq, dtype=np.float64)
    inv_freq = ROPE_BASE ** (-np.arange(0, dh, 2, dtype=np.float64) / dh)
    ang = pos[:, None] * inv_freq[None, :]
    cos = np.cos(ang)
    sin = np.sin(ang)
    cos2 = np.concatenate([cos, cos], axis=-1)
    sin2 = np.concatenate([-sin, sin], axis=-1)
    log_gamma = np.log1p(-np.exp2(-5.0 - np.arange(nh, dtype=np.float64)))
    idx = np.arange(c, dtype=np.float64)
    diff = idx[:, None] - idx[None, :]
    decay = np.where(diff[None] >= 0,
                     np.exp(np.maximum(diff, 0.0)[None] * log_gamma[:, None, None]), 0.0)
    zeta = np.exp((c - 1.0 - idx)[None] * log_gamma[:, None])
    xi = np.exp((idx + 1.0)[None] * log_gamma[:, None])
    gamma_chunk = np.exp(c * log_gamma)
    decay_l = decay.transpose(1, 0, 2).reshape(c, nh * c)
    zeta_t = np.broadcast_to(zeta[:, None, :], (nh, dh, c)).reshape(nh * dh, c)
    xi_l = np.broadcast_to(xi.T[:, :, None], (c, nh, dh)).reshape(c, nh * dh)
    gamma_l = np.broadcast_to(gamma_chunk[:, None], (nh, dh)).reshape(1, nh * dh)
    scale = dh ** -0.5
    tables = (cos2 * scale, sin2 * scale, cos2.T, sin2.T, decay_l, zeta_t, xi_l, gamma_l)
    return tuple(jnp.asarray(np.ascontiguousarray(a, dtype=np.float32)) for a in tables)


def _ret(q, k_t, v, g, tables, gain, layer, batch):
    t = q.shape[0]
    nblk = t // batch // RET_ROWS
    row = pl.BlockSpec((RET_ROWS, RET_WIDTH), lambda b, j: (b * nblk + j, 0))
    row_t = pl.BlockSpec((RET_WIDTH, RET_ROWS), lambda b, j: (0, b * nblk + j))
    rot = pl.BlockSpec((RET_ROWS, RET_HEAD_DIM), lambda b, j: (j, 0))
    rot_t = pl.BlockSpec((RET_HEAD_DIM, RET_ROWS), lambda b, j: (0, j))
    const = lambda shape: pl.BlockSpec(shape, lambda b, j: (0, 0), pipeline_mode=pl.Buffered(1))
    return pl.pallas_call(
        _ret_kernel,
        grid=(batch, nblk),
        in_specs=[row, row_t, row, row, rot, rot, rot_t, rot_t,
                  const((RET_CHUNK, RET_WIDTH)), const((RET_WIDTH, RET_CHUNK)),
                  const((RET_CHUNK, RET_WIDTH)), const((1, RET_WIDTH)),
                  _resident((1, RET_WIDTH), layer)],
        out_specs=row,
        out_shape=jax.ShapeDtypeStruct((t, RET_WIDTH), BF16),
        scratch_shapes=[pltpu.VMEM((RET_HEAD_DIM, RET_WIDTH), F32),
                        pltpu.VMEM((RET_CHUNK, RET_WIDTH), F32)],
        compiler_params=_params(32, 2),
        name="ret",
    )(q, k_t, v, g, *tables, gain)


def _block_diag(w):
    depth, n, d, e = w.shape
    eye = jnp.eye(n, dtype=w.dtype)
    return jnp.einsum('lnde,nm->lndme', w, eye).reshape(depth, n * d, n * e)


def kernel(x, ffn1_norm, ffn1_w_gate, ffn1_w_up, ffn1_w_down, mix_norm, w_in, s5_lambda_re, s5_lambda_im, s5_log_step, s5_b_re, s5_b_im, s5_c_re, s5_c_im, s5_d, s5_w_glu, s5_b_glu, s5_out_norm, ret_out_norm, lru_conv_w, lru_conv_b, lru_w_a, lru_b_a, lru_w_x, lru_b_x, lru_lambda, lru_out_norm, w_out, ffn2_norm, ffn2_w_gate, ffn2_w_up, ffn2_w_down, final_norm):
    batch, seq, d = x.shape
    depth = w_in.shape[0]
    t = batch * seq
    assert d == D_MODEL and seq % max(S5_STATE_ROWS, LRU_ROWS, RET_ROWS) == 0
    assert t % max(FFN_ROWS, PROJ_ROWS, S5_OUT_ROWS) == 0

    row3 = lambda a: a.reshape(depth, 1, a.shape[-1])
    bf = lambda a: a.astype(BF16)

    f1 = (row3(ffn1_norm), ffn1_w_gate, ffn1_w_up, ffn1_w_down)
    f2 = (row3(ffn2_norm), ffn2_w_gate, ffn2_w_up, ffn2_w_down)
    w_in_b = bf(w_in)
    k_off = sum(IN_SECTIONS[:K_SECTION])
    w_k = lax.optimization_barrier(w_in[:, :, k_off:k_off + RET_WIDTH])
    w_k_t = bf(jnp.swapaxes(w_k, 1, 2))
    w_out_b = bf(w_out)
    s5_wst, s5_m, s5_vt, s5_consts = _s5_prepare(s5_lambda_re, s5_lambda_im, s5_log_step,
                                                 s5_b_re, s5_b_im, s5_c_re, s5_c_im)
    s5_glu = bf(s5_w_glu)
    lru_wax = bf(jnp.concatenate([_block_diag(lru_w_a), _block_diag(lru_w_x)], axis=-1))
    lru_bax = row3(jnp.concatenate([lru_b_a, lru_b_x], axis=-1))
    tables = _ret_tables(seq)

    xt = x.reshape(t, d)
    for l in range(depth):
        xt = _ffn(xt, *f1, l)
        u, q, k_t, v, g, xl, gl = _inproj(xt, row3(mix_norm), w_in_b, w_k_t, l)
        uc, sp = _s5_state(u, s5_wst, s5_consts, l, batch)
        y_s5 = _s5_out(u, uc, sp, s5_m, s5_vt, row3(s5_d), s5_glu, row3(s5_b_glu),
                       row3(s5_out_norm), l)
        y_ret = _ret(q, k_t, v, g, tables, row3(ret_out_norm), l, batch)
        y_lru = _lru(xl, gl, lru_conv_w, row3(lru_conv_b), lru_wax, lru_bax, row3(lru_lambda),
                     row3(lru_out_norm), l, batch)
        last = final_norm.reshape(1, d) if l == depth - 1 else None
        xt = _ffn(xt, *f2, l, mix=(y_s5, y_ret, y_lru, w_out_b), final_gain=last)
    return xt.reshape(batch, seq, d)
```

```python
import functools
import math

import jax
import jax.numpy as jnp
import numpy as np
from jax import lax
from jax.experimental import pallas as pl
from jax.experimental.pallas import tpu as pltpu

F32 = jnp.float32
BF16 = jnp.bfloat16

D_MODEL = 1024
D_FF = 2816
S5_WIDTH = 256
S5_GROUP_DIM = 16
S5_GROUPS = 16
S5_STATE = 64
S5_CH = S5_GROUPS * S5_STATE
RET_WIDTH = 512
RET_HEAD_DIM = 128
RET_HEADS = 4
RET_CHUNK = 128
LRU_WIDTH = 256
LRU_BLOCKS = 4
LRU_BLOCK_DIM = 64
CONV_WIDTH = 4
LRU_C = 8.0
ROPE_BASE = 10000.0
NORM_EPS = 1e-6
IN_SECTIONS = (S5_WIDTH, RET_WIDTH, RET_WIDTH, RET_WIDTH, RET_WIDTH, LRU_WIDTH, LRU_WIDTH)
IN_WIDTH = sum(IN_SECTIONS)

SUBLANES = 8
LANES = 128
MIB = 1024 * 1024

FFN_ROWS = 1024
FFN_CHUNK = 256
PROJ_ROWS = 512
S5_STATE_ROWS = 4096
S5_OUT_ROWS = 4096
LRU_ROWS = 1024
LRU_UNROLL = 4
RET_ROWS = 1024
RET_UNROLL = 2


def _params(vmem_mib, n_axes):
    return pltpu.CompilerParams(
        dimension_semantics=("arbitrary",) * n_axes,
        vmem_limit_bytes=vmem_mib * MIB)


def _rms_rows(x):
    return x * lax.rsqrt(jnp.mean(x * x, axis=-1, keepdims=True) + NORM_EPS)


def _gelu_tanh(x):
    c = math.sqrt(2.0 / math.pi)
    return x * (0.5 * (1.0 + jnp.tanh(c * (x + 0.044715 * (x * x * x)))))


def _sigmoid(x):
    return 0.5 + 0.5 * jnp.tanh(0.5 * x)


def _silu(x):
    h = 0.5 * x
    return h + h * jnp.tanh(h)


def _dot(a, b):
    return jnp.dot(a, b, preferred_element_type=F32)


def _resident(shape, layer):
    nd = len(shape)
    return pl.BlockSpec((None,) + tuple(shape), lambda *_: (layer,) + (0,) * nd,
                        pipeline_mode=pl.Buffered(1))


def _ffn_kernel(*refs, layer, with_outproj, with_final_norm):
    refs = list(refs)
    wg_s, wu_s, wd_s, stage_g, stage_u, stage_d, sem = refs[-7:]
    o_ref = refs[-8]
    refs = refs[:-8]
    x_ref = refs.pop(0)
    if with_outproj:
        ys_ref, yr_ref, yl_ref, wo_ref = refs[:4]
        refs = refs[4:]
    g_ref, wg_hbm, wu_hbm, wd_hbm = refs[:4]

    x = x_ref[...]
    if with_outproj:
        x = x + _dot(ys_ref[...], wo_ref[0:S5_WIDTH, :])
        x = x + _dot(yr_ref[...], wo_ref[S5_WIDTH:S5_WIDTH + RET_WIDTH, :])
        x = x + _dot(yl_ref[...], wo_ref[S5_WIDTH + RET_WIDTH:, :])
    h = (_rms_rows(x) * g_ref[...]).astype(BF16)

    nchunk = D_FF // FFN_CHUNK
    chunks = [slice(c * FFN_CHUNK, (c + 1) * FFN_CHUNK) for c in range(nchunk)]

    def copy(which, c):
        sl = chunks[c]
        if which == 0:
            return pltpu.make_async_copy(wg_hbm.at[layer, :, sl], stage_g.at[c % 2], sem.at[0, c % 2])
        if which == 1:
            return pltpu.make_async_copy(wu_hbm.at[layer, :, sl], stage_u.at[c % 2], sem.at[1, c % 2])
        return pltpu.make_async_copy(wd_hbm.at[layer, sl, :], stage_d.at[0], sem.at[2, 0])

    @pl.when(pl.program_id(0) == 0)
    def _():
        for which, ahead in ((0, 2), (1, 2), (2, 1)):
            for c in range(ahead):
                copy(which, c).start()
        for c, sl in enumerate(chunks):
            copy(0, c).wait()
            wg_s[:, sl] = stage_g[c % 2].astype(BF16)
            if c + 2 < nchunk:
                copy(0, c + 2).start()
            copy(1, c).wait()
            wu_s[:, sl] = stage_u[c % 2].astype(BF16)
            if c + 2 < nchunk:
                copy(1, c + 2).start()
            copy(2, c).wait()
            wd_s[sl, :] = stage_d[0].astype(BF16)
            if c + 1 < nchunk:
                copy(2, c + 1).start()

    acc = jnp.zeros(x.shape, F32)
    for sl in chunks:
        act = (_silu(_dot(h, wg_s[:, sl])) * _dot(h, wu_s[:, sl])).astype(BF16)
        acc = acc + _dot(act, wd_s[sl, :])
    y = x + 0.5 * acc
    if with_final_norm:
        y = _rms_rows(y) * refs[4][...]
    o_ref[...] = y


def _ffn(x, gain, wg, wu, wd, layer, mix=None, final_gain=None):
    t = x.shape[0]
    rows = FFN_ROWS
    spec = lambda w: pl.BlockSpec((rows, w), lambda i: (i, 0))
    hbm = pl.BlockSpec(memory_space=pl.ANY)
    operands = [x]
    in_specs = [spec(D_MODEL)]
    if mix is not None:
        y_s5, y_ret, y_lru, w_out = mix
        operands += [y_s5, y_ret, y_lru, w_out]
        in_specs += [spec(S5_WIDTH), spec(RET_WIDTH), spec(LRU_WIDTH),
                     _resident((D_MODEL, D_MODEL), layer)]
    operands += [gain, wg, wu, wd]
    in_specs += [_resident((1, D_MODEL), layer), hbm, hbm, hbm]
    if final_gain is not None:
        operands.append(final_gain)
        in_specs.append(pl.BlockSpec((1, D_MODEL), lambda i: (0, 0)))
    return pl.pallas_call(
        functools.partial(_ffn_kernel, layer=layer, with_outproj=mix is not None,
                          with_final_norm=final_gain is not None),
        grid=(t // rows,),
        in_specs=in_specs,
        out_specs=spec(D_MODEL),
        out_shape=jax.ShapeDtypeStruct(x.shape, F32),
        scratch_shapes=[pltpu.VMEM((D_MODEL, D_FF), BF16),
                        pltpu.VMEM((D_MODEL, D_FF), BF16),
                        pltpu.VMEM((D_FF, D_MODEL), BF16),
                        pltpu.VMEM((2, D_MODEL, FFN_CHUNK), F32),
                        pltpu.VMEM((2, D_MODEL, FFN_CHUNK), F32),
                        pltpu.VMEM((1, FFN_CHUNK, D_MODEL), F32),
                        pltpu.SemaphoreType.DMA((3, 2))],
        compiler_params=_params(58, 1),
        name="ffn_mix" if mix is not None else "ffn",
    )(*operands)


K_SECTION = 2


def _inproj_kernel(x_ref, g_ref, w_ref, wkt_ref, *out_refs):
    h = (_rms_rows(x_ref[...]) * g_ref[...]).astype(BF16)
    off = 0
    for n, (o_ref, width) in enumerate(zip(out_refs, IN_SECTIONS)):
        if n == K_SECTION:
            o_ref[...] = lax.dot_general(wkt_ref[...], h, (((1,), (1,)), ((), ())),
                                         preferred_element_type=F32)
        else:
            o_ref[...] = _dot(h, w_ref[:, off:off + width])
        off += width


def _inproj(x, gain, w_in, w_k_t, layer):
    t = x.shape[0]
    row = lambda w: pl.BlockSpec((PROJ_ROWS, w), lambda i: (i, 0))
    out_specs = [row(w) for w in IN_SECTIONS]
    out_shape = [jax.ShapeDtypeStruct((t, w), F32) for w in IN_SECTIONS]
    out_specs[K_SECTION] = pl.BlockSpec((RET_WIDTH, PROJ_ROWS), lambda i: (0, i))
    out_shape[K_SECTION] = jax.ShapeDtypeStruct((RET_WIDTH, t), F32)
    return pl.pallas_call(
        _inproj_kernel,
        grid=(t // PROJ_ROWS,),
        in_specs=[row(D_MODEL),
                  _resident((1, D_MODEL), layer),
                  _resident((D_MODEL, IN_WIDTH), layer),
                  _resident((RET_WIDTH, D_MODEL), layer)],
        out_specs=out_specs,
        out_shape=out_shape,
        compiler_params=_params(40, 1),
        name="inproj",
    )(x, gain, w_in, w_k_t)


S5_CHUNK = 8
S5_CAT = S5_CHUNK * S5_WIDTH
S5_CONST_ROWS = 8 * SUBLANES


def _cmul(a, b):
    return a[0] * b[0] - a[1] * b[1], a[0] * b[1] + a[1] * b[0]


def _s5_prep_kernel(lr_ref, li_ref, ls_ref, er_ref, ei_ref, ctr_ref, cti_ref, cst_ref,
                    wst_ref, kk_ref, vt_ref, c_ref):
    k = pl.program_id(1)
    lr = lr_ref[...]
    li = li_ref[...]
    step = jnp.exp(ls_ref[...])
    ar = lr * step
    ai = li * step
    mag = jnp.exp(ar)
    p1 = (mag * jnp.cos(ai), mag * jnp.sin(ai))
    nr = p1[0] - 1.0
    den = lr * lr + li * li
    f = ((nr * lr + p1[1] * li) / den, (p1[1] * lr - nr * li) / den)
    powers = [(jnp.ones_like(lr), jnp.zeros_like(lr)), p1]
    for _ in range(2, S5_CHUNK + 1):
        powers.append(_cmul(powers[-1], p1))

    def pick(n):
        out = powers[0]
        for i in range(1, S5_CHUNK + 1):
            out = (jnp.where(n == i, powers[i][0], out[0]), jnp.where(n == i, powers[i][1], out[1]))
        return out

    fp = _cmul(f, pick(k))
    er = er_ref[...]
    ei = ei_ref[...]
    w = jnp.concatenate([fp[0] * er - fp[1] * ei, fp[0] * ei + fp[1] * er], axis=1)
    wb = w.astype(BF16)
    wst_ref[...] = wb
    kk_ref[...] = _dot(wb, cst_ref[...]).astype(BF16)
    a, b = pick(k + 1)
    ctr = ctr_ref[...]
    cti = cti_ref[...]
    vt_ref[:, :S5_CH] = (ctr * a - cti * b).astype(BF16)
    vt_ref[:, S5_CH:] = (-(ctr * b) - cti * a).astype(BF16)

    @pl.when(k == 0)
    def _():
        q = [powers[S5_CHUNK]]
        for _ in range(1, SUBLANES):
            q.append(_cmul(q[-1], q[0]))
        row = lax.broadcasted_iota(jnp.int32, (SUBLANES, S5_CH), 0)
        zero = jnp.zeros((SUBLANES, S5_CH), F32)
        for s, d in enumerate((1, 2, 4)):
            for part in range(2):
                base = (2 * s + part) * SUBLANES
                c_ref[base:base + SUBLANES, :] = jnp.where(
                    row >= d, jnp.broadcast_to(q[d - 1][part], (SUBLANES, S5_CH)), zero)
        for part in range(2):
            tile = zero
            for r in range(SUBLANES):
                tile = jnp.where(row == r, jnp.broadcast_to(q[r][part], (SUBLANES, S5_CH)), tile)
            base = (6 + part) * SUBLANES
            c_ref[base:base + SUBLANES, :] = tile


def _s5_prepare(lam_re, lam_im, log_step, b_re, b_im, c_re, c_im):
    depth = lam_re.shape[0]
    def embed(a):
        n, m = a.shape[2:]
        same = np.equal.outer(np.repeat(np.arange(S5_GROUPS), n), np.arange(S5_GROUPS))
        tiled = jnp.broadcast_to(a.reshape(depth, S5_GROUPS * n, 1, m),
                                 (depth, S5_GROUPS * n, S5_GROUPS, m))
        return jnp.where(same[None, :, :, None], tiled, 0.0).reshape(
            depth, S5_GROUPS * n, S5_GROUPS * m)

    b_rows = lambda b: embed(jnp.swapaxes(b, 2, 3))
    c_rows = embed
    c_cols = lambda c: embed(jnp.swapaxes(c, 2, 3))
    c_stack = jnp.concatenate([c_cols(c_re), -c_cols(c_im)], axis=1).astype(BF16)
    vec = lambda a: a.reshape(depth, 1, S5_CH)
    ls = jnp.broadcast_to(log_step[:, :, None], (depth, S5_GROUPS, S5_STATE))
    vspec = pl.BlockSpec((None, 1, S5_CH), lambda l, k: (l, 0, 0))
    mspec = pl.BlockSpec((None, S5_WIDTH, S5_CH), lambda l, k: (l, 0, 0))
    reverse = lambda l, k: (l, S5_CHUNK - 1 - k, 0)
    return pl.pallas_call(
        _s5_prep_kernel,
        grid=(depth, S5_CHUNK),
        in_specs=[vspec, vspec, vspec, mspec, mspec, mspec, mspec,
                  pl.BlockSpec((None, 2 * S5_CH, S5_WIDTH), lambda l, k: (l, 0, 0))],
        out_specs=[pl.BlockSpec((None, S5_WIDTH, 2 * S5_CH), reverse),
                   pl.BlockSpec((None, S5_WIDTH, S5_WIDTH), reverse),
                   pl.BlockSpec((None, S5_WIDTH, 2 * S5_CH), lambda l, k: (l, k, 0)),
                   pl.BlockSpec((None, S5_CONST_ROWS, S5_CH), lambda l, k: (l, 0, 0))],
        out_shape=[jax.ShapeDtypeStruct((depth, S5_CAT, 2 * S5_CH), BF16),
                   jax.ShapeDtypeStruct((depth, S5_CAT, S5_WIDTH), BF16),
                   jax.ShapeDtypeStruct((depth, S5_CAT, 2 * S5_CH), BF16),
                   jax.ShapeDtypeStruct((depth, S5_CONST_ROWS, S5_CH), F32)],
        compiler_params=_params(40, 2),
        name="s5_prep",
    )(vec(lam_re), vec(lam_im), vec(ls), b_rows(b_re), b_rows(b_im), c_rows(c_re), c_rows(c_im),
      c_stack)


def _s5_state_kernel(ulo_ref, uhi_ref, wst_ref, k_ref, uc_ref, sp_ref, ds_scr, s_scr):
    j = pl.program_id(1)
    nch = ulo_ref.shape[0] // S5_CHUNK
    ncol = S5_CH // LANES
    cols = [(slice(cb * LANES, (cb + 1) * LANES),
             slice(S5_CH + cb * LANES, S5_CH + (cb + 1) * LANES)) for cb in range(ncol)]

    @pl.when(j == 0)
    def _():
        s_scr[0:SUBLANES, :] = jnp.zeros((SUBLANES, 2 * S5_CH), F32)

    @pl.when(j > 0)
    def _():
        s_scr[0:SUBLANES, :] = s_scr[nch:nch + SUBLANES, :]

    uc = jnp.concatenate([half[pl.ds(jj, nch, stride=S5_CHUNK), :]
                          for jj in range(S5_CHUNK) for half in (ulo_ref, uhi_ref)],
                         axis=1).astype(BF16)
    uc_ref[...] = uc
    ds_scr[...] = _dot(uc, wst_ref[...])

    def tile_body(i, carry):
        off = pl.multiple_of(i * SUBLANES, SUBLANES)
        loaded = [(ds_scr[pl.ds(off, SUBLANES), re], ds_scr[pl.ds(off, SUBLANES), im])
                  for re, im in cols]
        done = []
        for cb, (br, bi) in enumerate(loaded):
            re = cols[cb][0]
            for s, d in enumerate((1, 2, 4)):
                cr = k_ref[(2 * s) * SUBLANES:(2 * s + 1) * SUBLANES, re]
                ci = k_ref[(2 * s + 1) * SUBLANES:(2 * s + 2) * SUBLANES, re]
                rr = pltpu.roll(br, d, 0)
                ri = pltpu.roll(bi, d, 0)
                br, bi = br + (cr * rr - ci * ri), bi + (cr * ri + ci * rr)
            pr, pi = carry[cb], carry[ncol + cb]
            wr = k_ref[6 * SUBLANES:7 * SUBLANES, re]
            wi = k_ref[7 * SUBLANES:8 * SUBLANES, re]
            done.append((br + (wr * pr - wi * pi), bi + (wr * pi + wi * pr)))
        for (re, im), (sr, si) in zip(cols, done):
            s_scr[pl.ds(off + SUBLANES, SUBLANES), re] = sr
            s_scr[pl.ds(off + SUBLANES, SUBLANES), im] = si
        return tuple(last(sr) for sr, _ in done) + tuple(last(si) for _, si in done)

    last = lambda a: jnp.broadcast_to(a[SUBLANES - 1:SUBLANES, :], (SUBLANES, LANES))
    init = (tuple(last(s_scr[0:SUBLANES, re]) for re, _ in cols)
            + tuple(last(s_scr[0:SUBLANES, im]) for _, im in cols))
    lax.fori_loop(0, nch // SUBLANES, tile_body, init)

    sp_ref[...] = s_scr[SUBLANES - 1:SUBLANES - 1 + nch, :].astype(BF16)


def _s5_state(u, wst, consts, layer, batch):
    t = u.shape[0]
    nblk = t // batch // S5_STATE_ROWS
    nch = S5_STATE_ROWS // S5_CHUNK
    chunk_rows = pl.BlockSpec((nch, S5_CAT), lambda b, j: (b * nblk + j, 0))
    return pl.pallas_call(
        _s5_state_kernel,
        grid=(batch, nblk),
        in_specs=[pl.BlockSpec((S5_STATE_ROWS, LANES), lambda b, j: (b * nblk + j, 0)),
                  pl.BlockSpec((S5_STATE_ROWS, LANES), lambda b, j: (b * nblk + j, 1)),
                  _resident((S5_CAT, 2 * S5_CH), layer),
                  _resident((S5_CONST_ROWS, S5_CH), layer)],
        out_specs=[chunk_rows, chunk_rows],
        out_shape=[jax.ShapeDtypeStruct((t // S5_CHUNK, S5_CAT), BF16),
                   jax.ShapeDtypeStruct((t // S5_CHUNK, 2 * S5_CH), BF16)],
        scratch_shapes=[pltpu.VMEM((nch, 2 * S5_CH), F32),
                        pltpu.VMEM((nch + SUBLANES, 2 * S5_CH), F32)],
        compiler_params=_params(44, 2),
        name="s5_state",
    )(u, u, wst, consts)


def _s5_out_kernel(u_ref, uc_ref, sp_ref, krev_ref, vt_ref, d_ref, wglu_ref, bglu_ref, gain_ref,
                   o_ref, ylo_scr, yhi_scr):
    nch = uc_ref.shape[0]
    sp = sp_ref[...]
    for r in range(S5_CHUNK):
        cols = slice(r * S5_WIDTH, (r + 1) * S5_WIDTH)
        used = (r + 1) * S5_WIDTH
        y_r = _dot(uc_ref[:, :used], krev_ref[S5_CAT - used:, :])
        y_r = y_r + lax.dot_general(sp, vt_ref[cols, :], (((1,), (1,)), ((), ())),
                                    preferred_element_type=F32)
        ylo_scr[pl.ds(r, nch, stride=S5_CHUNK), :] = y_r[:, :LANES]
        yhi_scr[pl.ds(r, nch, stride=S5_CHUNK), :] = y_r[:, LANES:]
    y = jnp.concatenate([ylo_scr[...], yhi_scr[...]], axis=1) + d_ref[...] * u_ref[...]
    y = _gelu_tanh(y)
    y = y * _sigmoid(_dot(y.astype(BF16), wglu_ref[...]) + bglu_ref[...])
    o_ref[...] = (_rms_rows(y) * gain_ref[...]).astype(BF16)


def _s5_out(u, uc, sp, krev, vt, d_skip, w_glu, b_glu, gain, layer):
    t = u.shape[0]
    nch = S5_OUT_ROWS // S5_CHUNK
    row = pl.BlockSpec((S5_OUT_ROWS, S5_WIDTH), lambda i: (i, 0))
    chunk_rows = pl.BlockSpec((nch, S5_CAT), lambda i: (i, 0))
    return pl.pallas_call(
        _s5_out_kernel,
        grid=(t // S5_OUT_ROWS,),
        in_specs=[row, chunk_rows, chunk_rows,
                  _resident((S5_CAT, S5_WIDTH), layer),
                  _resident((S5_CAT, 2 * S5_CH), layer),
                  _resident((1, S5_WIDTH), layer),
                  _resident((S5_WIDTH, S5_WIDTH), layer),
                  _resident((1, S5_WIDTH), layer),
                  _resident((1, S5_WIDTH), layer)],
        out_specs=row,
        out_shape=jax.ShapeDtypeStruct((t, S5_WIDTH), BF16),
        scratch_shapes=[pltpu.VMEM((S5_OUT_ROWS, LANES), F32),
                        pltpu.VMEM((S5_OUT_ROWS, LANES), F32)],
        compiler_params=_params(48, 1),
        name="s5_out",
    )(u, uc, sp, krev, vt, d_skip, w_glu, b_glu, gain)


def _lru_kernel(x_ref, gate_ref, cw_ref, cb_ref, wax_ref, bax_ref, lam_ref, gain_ref, o_ref,
                ext_scr, a_scr, b_scr, h_scr, carry_scr):
    j = pl.program_id(1)
    rows = x_ref.shape[0]
    pad = SUBLANES

    @pl.when(j == 0)
    def _():
        ext_scr[0:pad, :] = jnp.zeros((pad, LRU_WIDTH), F32)
        carry_scr[...] = jnp.zeros(carry_scr.shape, F32)

    @pl.when(j > 0)
    def _():
        ext_scr[0:pad, :] = ext_scr[rows:rows + pad, :]

    ext_scr[pad:rows + pad, :] = x_ref[...]
    xc = cb_ref[...] + cw_ref[CONV_WIDTH - 1:CONV_WIDTH, :] * ext_scr[pad:rows + pad, :]
    for lag in range(1, CONV_WIDTH):
        tap = cw_ref[CONV_WIDTH - 1 - lag:CONV_WIDTH - lag, :]
        xc = xc + tap * ext_scr[pad - lag:rows + pad - lag, :]

    gates = _sigmoid(_dot(xc.astype(BF16), wax_ref[...]) + bax_ref[...])
    r = gates[:, :LRU_WIDTH]
    i = gates[:, LRU_WIDTH:]
    z = -lam_ref[...]
    softplus = jnp.maximum(z, 0.0) + jnp.log1p(jnp.exp(-jnp.abs(z)))
    a = jnp.exp((-LRU_C) * r * softplus)
    a_scr[...] = a
    b_scr[...] = jnp.sqrt(1.0 - a * a) * (i * xc)

    row = lax.broadcasted_iota(jnp.int32, (SUBLANES, LRU_WIDTH), 0)

    def tile_body(t, prev):
        off = pl.multiple_of(t * SUBLANES, SUBLANES)
        at = a_scr[pl.ds(off, SUBLANES), :]
        bt = b_scr[pl.ds(off, SUBLANES), :]
        for d in (1, 2, 4):
            keep = row >= d
            ar = jnp.where(keep, pltpu.roll(at, d, 0), 1.0)
            br = jnp.where(keep, pltpu.roll(bt, d, 0), 0.0)
            bt = at * br + bt
            at = at * ar
        h = at * prev + bt
        h_scr[pl.ds(off, SUBLANES), :] = h
        return jnp.broadcast_to(h[SUBLANES - 1:SUBLANES, :], (SUBLANES, LRU_WIDTH))

    carry_scr[...] = lax.fori_loop(0, rows // SUBLANES, tile_body, carry_scr[...],
                                   unroll=LRU_UNROLL)

    y = h_scr[...] * _gelu_tanh(gate_ref[...])
    o_ref[...] = (_rms_rows(y) * gain_ref[...]).astype(BF16)


def _lru(x, gate, conv_w, conv_b, wax, bax, lam, gain, layer, batch):
    t = x.shape[0]
    nblk = t // batch // LRU_ROWS
    row = pl.BlockSpec((LRU_ROWS, LRU_WIDTH), lambda b, j: (b * nblk + j, 0))
    return pl.pallas_call(
        _lru_kernel,
        grid=(batch, nblk),
        in_specs=[row, row,
                  _resident((CONV_WIDTH, LRU_WIDTH), layer),
                  _resident((1, LRU_WIDTH), layer),
                  _resident((LRU_WIDTH, 2 * LRU_WIDTH), layer),
                  _resident((1, 2 * LRU_WIDTH), layer),
                  _resident((1, LRU_WIDTH), layer),
                  _resident((1, LRU_WIDTH), layer)],
        out_specs=row,
        out_shape=jax.ShapeDtypeStruct((t, LRU_WIDTH), BF16),
        scratch_shapes=[pltpu.VMEM((LRU_ROWS + SUBLANES, LRU_WIDTH), F32),
                        pltpu.VMEM((LRU_ROWS, LRU_WIDTH), F32),
                        pltpu.VMEM((LRU_ROWS, LRU_WIDTH), F32),
                        pltpu.VMEM((LRU_ROWS, LRU_WIDTH), F32),
                        pltpu.VMEM((SUBLANES, LRU_WIDTH), F32)],
        compiler_params=_params(32, 2),
        name="lru",
    )(x, gate, conv_w, conv_b, wax, bax, lam, gain)


def _ret_kernel(q_ref, kt_ref, v_ref, g_ref, cos_ref, sin_ref, cost_ref, sint_ref, decay_ref,
                zetat_ref, xi_ref, gamma_ref, gain_ref, o_ref, state_scr, y_scr):
    j = pl.program_id(1)

    @pl.when(j == 0)
    def _():
        state_scr[...] = jnp.zeros(state_scr.shape, F32)

    half = RET_HEAD_DIM // 2

    def chunk(start):
        rows = pl.ds(start, RET_CHUNK)
        cs = cos_ref[rows, :]
        sn = sin_ref[rows, :]
        cst = cost_ref[:, rows]
        snt = sint_ref[:, rows]
        ssq = jnp.zeros((RET_CHUNK, 1), F32)
        for h in range(RET_HEADS):
            cols = slice(h * RET_HEAD_DIM, (h + 1) * RET_HEAD_DIM)
            qh = q_ref[rows, cols]
            kt = kt_ref[cols, rows]
            vh = v_ref[rows, cols].astype(BF16)
            qb = (qh * cs + pltpu.roll(qh, half, 1) * sn).astype(BF16)
            kr = kt * cst + jnp.concatenate([kt[half:], kt[:half]], axis=0) * snt
            scores = _dot(qb, kr.astype(BF16)) * decay_ref[:, cols]
            inner = _dot(scores.astype(BF16), vh)
            st = state_scr[:, cols]
            cross = _dot(qb, st.astype(BF16)) * xi_ref[:, cols]
            kz = (kr * zetat_ref[cols, :]).astype(BF16)
            state_scr[:, cols] = gamma_ref[:, cols] * st + _dot(kz, vh)
            yh = _silu(g_ref[rows, cols]) * _rms_rows(inner + cross)
            ssq = ssq + jnp.sum(yh * yh, axis=-1, keepdims=True)
            y_scr[:, cols] = yh
        inv = lax.rsqrt(ssq * (1.0 / RET_WIDTH) + NORM_EPS)
        o_ref[rows, :] = ((y_scr[...] * inv) * gain_ref[...]).astype(BF16)

    def body(i, carry):
        for c in range(RET_UNROLL):
            chunk(pl.multiple_of((i * RET_UNROLL + c) * RET_CHUNK, RET_CHUNK))
        return carry

    lax.fori_loop(0, q_ref.shape[0] // (RET_CHUNK * RET_UNROLL), body, 0)


def _ret_tables(seq):
    dh, c, nh = RET_HEAD_DIM, RET_CHUNK, RET_HEADS
    pos = np.arange(seq, dtype=np.float64)
    inv_freq = ROPE_BASE ** (-np.arange(0, dh, 2, dtype=np.float64) / dh)
    ang = pos[:, None] * inv_freq[None, :]
    cos = np.cos(ang)
    sin = np.sin(ang)
    cos2 = np.concatenate([cos, cos], axis=-1)
    sin2 = np.concatenate([-sin, sin], axis=-1)
    log_gamma = np.log1p(-np.exp2(-5.0 - np.arange(nh, dtype=np.float64)))
    idx = np.arange(c, dtype=np.float64)
    diff = idx[:, None] - idx[None, :]
    decay = np.where(diff[None] >= 0,
                     np.exp(np.maximum(diff, 0.0)[None] * log_gamma[:, None, None]), 0.0)
    zeta = np.exp((c - 1.0 - idx)[None] * log_gamma[:, None])
    xi = np.exp((idx + 1.0)[None] * log_gamma[:, None])
    gamma_chunk = np.exp(c * log_gamma)
    decay_l = decay.transpose(1, 0, 2).reshape(c, nh * c)
    zeta_t = np.broadcast_to(zeta[:, None, :], (nh, dh, c)).reshape(nh * dh, c)
    xi_l = np.broadcast_to(xi.T[:, :, None], (c, nh, dh)).reshape(c, nh * dh)
    gamma_l = np.broadcast_to(gamma_chunk[:, None], (nh, dh)).reshape(1, nh * dh)
    scale = dh ** -0.5
    tables = (cos2 * scale, sin2 * scale, cos2.T, sin2.T, decay_l, zeta_t, xi_l, gamma_l)
    return tuple(jnp.asarray(np.ascontiguousarray(a, dtype=np.float32)) for a in tables)


def _ret(q, k_t, v, g, tables, gain, layer, batch):
    t = q.shape[0]
    nblk = t // batch // RET_ROWS
    row = pl.BlockSpec((RET_ROWS, RET_WIDTH), lambda b, j: (b * nblk + j, 0))
    row_t = pl.BlockSpec((RET_WIDTH, RET_ROWS), lambda b, j: (0, b * nblk + j))
    rot = pl.BlockSpec((RET_ROWS, RET_HEAD_DIM), lambda b, j: (j, 0))
    rot_t = pl.BlockSpec((RET_HEAD_DIM, RET_ROWS), lambda b, j: (0, j))
    const = lambda shape: pl.BlockSpec(shape, lambda b, j: (0, 0), pipeline_mode=pl.Buffered(1))
    return pl.pallas_call(
        _ret_kernel,
        grid=(batch, nblk),
        in_specs=[row, row_t, row, row, rot, rot, rot_t, rot_t,
                  const((RET_CHUNK, RET_WIDTH)), const((RET_WIDTH, RET_CHUNK)),
                  const((RET_CHUNK, RET_WIDTH)), const((1, RET_WIDTH)),
                  _resident((1, RET_WIDTH), layer)],
        out_specs=row,
        out_shape=jax.ShapeDtypeStruct((t, RET_WIDTH), BF16),
        scratch_shapes=[pltpu.VMEM((RET_HEAD_DIM, RET_WIDTH), F32),
                        pltpu.VMEM((RET_CHUNK, RET_WIDTH), F32)],
        compiler_params=_params(32, 2),
        name="ret",
    )(q, k_t, v, g, *tables, gain)


def _block_diag(w):
    depth, n, d, e = w.shape
    eye = jnp.eye(n, dtype=w.dtype)
    return jnp.einsum('lnde,nm->lndme', w, eye).reshape(depth, n * d, n * e)


def kernel(x, ffn1_norm, ffn1_w_gate, ffn1_w_up, ffn1_w_down, mix_norm, w_in, s5_lambda_re, s5_lambda_im, s5_log_step, s5_b_re, s5_b_im, s5_c_re, s5_c_im, s5_d, s5_w_glu, s5_b_glu, s5_out_norm, ret_out_norm, lru_conv_w, lru_conv_b, lru_w_a, lru_b_a, lru_w_x, lru_b_x, lru_lambda, lru_out_norm, w_out, ffn2_norm, ffn2_w_gate, ffn2_w_up, ffn2_w_down, final_norm):
    batch, seq, d = x.shape
    depth = w_in.shape[0]
    t = batch * seq
    assert d == D_MODEL and seq % max(S5_STATE_ROWS, LRU_ROWS, RET_ROWS) == 0
    assert t % max(FFN_ROWS, PROJ_ROWS, S5_OUT_ROWS) == 0

    row3 = lambda a: a.reshape(depth, 1, a.shape[-1])
    bf = lambda a: a.astype(BF16)

    f1 = (row3(ffn1_norm), ffn1_w_gate, ffn1_w_up, ffn1_w_down)
    f2 = (row3(ffn2_norm), ffn2_w_gate, ffn2_w_up, ffn2_w_down)
    w_in_b = bf(w_in)
    k_off = sum(IN_SECTIONS[:K_SECTION])
    w_k = lax.optimization_barrier(w_in[:, :, k_off:k_off + RET_WIDTH])
    w_k_t = bf(jnp.swapaxes(w_k, 1, 2))
    w_out_b = bf(w_out)
    s5_wst, s5_krev, s5_vt, s5_consts = _s5_prepare(s5_lambda_re, s5_lambda_im, s5_log_step,
                                                 s5_b_re, s5_b_im, s5_c_re, s5_c_im)
    s5_glu = bf(s5_w_glu)
    lru_wax = bf(jnp.concatenate([_block_diag(lru_w_a), _block_diag(lru_w_x)], axis=-1))
    lru_bax = row3(jnp.concatenate([lru_b_a, lru_b_x], axis=-1))
    tables = _ret_tables(seq)

    xt = x.reshape(t, d)
    for l in range(depth):
        xt = _ffn(xt, *f1, l)
        u, q, k_t, v, g, xl, gl = _inproj(xt, row3(mix_norm), w_in_b, w_k_t, l)
        uc, sp = _s5_state(u, s5_wst, s5_consts, l, batch)
        y_s5 = _s5_out(u, uc, sp, s5_krev, s5_vt, row3(s5_d), s5_glu, row3(s5_b_glu),
                       row3(s5_out_norm), l)
        y_ret = _ret(q, k_t, v, g, tables, row3(ret_out_norm), l, batch)
        y_lru = _lru(xl, gl, lru_conv_w, row3(lru_conv_b), lru_wax, lru_bax, row3(lru_lambda),
                     row3(lru_out_norm), l, batch)
        last = final_norm.reshape(1, d) if l == depth - 1 else None
        xt = _ffn(xt, *f2, l, mix=(y_s5, y_ret, y_lru, w_out_b), final_gain=last)
    return xt.reshape(batch, seq, d)
```

```python
import functools
import math

import jax
import jax.numpy as jnp
import numpy as np
from jax import lax
from jax.experimental import pallas as pl
from jax.experimental.pallas import tpu as pltpu

F32 = jnp.float32
BF16 = jnp.bfloat16

D_MODEL = 1024
D_FF = 2816
S5_WIDTH = 256
S5_GROUP_DIM = 16
S5_GROUPS = 16
S5_STATE = 64
S5_CH = S5_GROUPS * S5_STATE
RET_WIDTH = 512
RET_HEAD_DIM = 128
RET_HEADS = 4
RET_CHUNK = 128
LRU_WIDTH = 256
LRU_BLOCKS = 4
LRU_BLOCK_DIM = 64
CONV_WIDTH = 4
LRU_C = 8.0
ROPE_BASE = 10000.0
NORM_EPS = 1e-6
IN_SECTIONS = (S5_WIDTH, RET_WIDTH, RET_WIDTH, RET_WIDTH, RET_WIDTH, LRU_WIDTH, LRU_WIDTH)
IN_WIDTH = sum(IN_SECTIONS)

SUBLANES = 8
LANES = 128
MIB = 1024 * 1024

FFN_ROWS = 1024
FFN_CHUNK = 256
FFN_LOAD_ROWS = 64
PROJ_ROWS = 1024
S5_STATE_ROWS = 4096
S5_OUT_ROWS = 4096
LRU_ROWS = 2048
LRU_UNROLL = 4
RET_ROWS = 1024
RET_UNROLL = 2


def _params(vmem_mib, n_axes):
    return pltpu.CompilerParams(
        dimension_semantics=("arbitrary",) * n_axes,
        vmem_limit_bytes=vmem_mib * MIB)


def _rms_rows(x):
    return x * lax.rsqrt(jnp.mean(x * x, axis=-1, keepdims=True) + NORM_EPS)


def _gelu_tanh(x):
    c = math.sqrt(2.0 / math.pi)
    return x * (0.5 * (1.0 + jnp.tanh(c * (x + 0.044715 * (x * x * x)))))


def _sigmoid(x):
    return 0.5 + 0.5 * jnp.tanh(0.5 * x)


def _silu(x):
    h = 0.5 * x
    return h + h * jnp.tanh(h)


def _dot(a, b):
    return jnp.dot(a, b, preferred_element_type=F32)


def _resident(shape, layer):
    nd = len(shape)
    return pl.BlockSpec((None,) + tuple(shape), lambda *_: (layer,) + (0,) * nd,
                        pipeline_mode=pl.Buffered(1))


def _ffn_kernel(*refs, layer, with_outproj, with_final_norm):
    refs = list(refs)
    wg_s, wu_s, wd_s, stage_g, stage_u, stage_d, sem = refs[-7:]
    o_ref = refs[-8]
    refs = refs[:-8]
    x_ref = refs.pop(0)
    if with_outproj:
        ys_ref, yr_ref, yl_ref, wo_ref = refs[:4]
        refs = refs[4:]
    g_ref, wg_hbm, wu_hbm, wd_hbm = refs[:4]

    x = x_ref[...]
    if with_outproj:
        x = x + _dot(ys_ref[...], wo_ref[0:S5_WIDTH, :])
        x = x + _dot(yr_ref[...], wo_ref[S5_WIDTH:S5_WIDTH + RET_WIDTH, :])
        x = x + _dot(yl_ref[...], wo_ref[S5_WIDTH + RET_WIDTH:, :])
    h = (_rms_rows(x) * g_ref[...]).astype(BF16)

    nchunk = D_FF // FFN_CHUNK
    chunks = [slice(c * FFN_CHUNK, (c + 1) * FFN_CHUNK) for c in range(nchunk)]

    streams = ((wg_hbm, wg_s, stage_g, 0), (wu_hbm, wu_s, stage_u, 1), (wd_hbm, wd_s, stage_d, 2))

    def slab(stream, c):
        src, dst, stage, row = stream
        nslot, rows = stage.shape[0], stage.shape[1]
        rs = slice(c * rows, (c + 1) * rows)
        cp = pltpu.make_async_copy(src.at[layer, rs, :], stage.at[c % nslot], sem.at[row, c % nslot])
        return cp, dst, rs, stage, c % nslot

    @pl.when(pl.program_id(0) == 0)
    def _():
        counts = [st[1].shape[0] // st[2].shape[1] for st in streams]
        for st in streams:
            for c in range(st[2].shape[0]):
                slab(st, c)[0].start()
        for c in range(max(counts)):
            for st, n in zip(streams, counts):
                if c < n:
                    cp, dst, rs, stage, slot = slab(st, c)
                    cp.wait()
                    dst[rs, :] = stage[slot].astype(BF16)
                    if c + stage.shape[0] < n:
                        slab(st, c + stage.shape[0])[0].start()

    acc = jnp.zeros(x.shape, F32)
    for sl in chunks:
        act = (_silu(_dot(h, wg_s[:, sl])) * _dot(h, wu_s[:, sl])).astype(BF16)
        acc = acc + _dot(act, wd_s[sl, :])
    y = x + 0.5 * acc
    if with_final_norm:
        y = _rms_rows(y) * refs[4][...]
    o_ref[...] = y


def _ffn(x, gain, wg, wu, wd, layer, mix=None, final_gain=None):
    t = x.shape[0]
    rows = FFN_ROWS
    spec = lambda w: pl.BlockSpec((rows, w), lambda i: (i, 0))
    hbm = pl.BlockSpec(memory_space=pl.ANY)
    operands = [x]
    in_specs = [spec(D_MODEL)]
    if mix is not None:
        y_s5, y_ret, y_lru, w_out = mix
        operands += [y_s5, y_ret, y_lru, w_out]
        in_specs += [spec(S5_WIDTH), spec(RET_WIDTH), spec(LRU_WIDTH),
                     _resident((D_MODEL, D_MODEL), layer)]
    operands += [gain, wg, wu, wd]
    in_specs += [_resident((1, D_MODEL), layer), hbm, hbm, hbm]
    if final_gain is not None:
        operands.append(final_gain)
        in_specs.append(pl.BlockSpec((1, D_MODEL), lambda i: (0, 0)))
    return pl.pallas_call(
        functools.partial(_ffn_kernel, layer=layer, with_outproj=mix is not None,
                          with_final_norm=final_gain is not None),
        grid=(t // rows,),
        in_specs=in_specs,
        out_specs=spec(D_MODEL),
        out_shape=jax.ShapeDtypeStruct(x.shape, F32),
        scratch_shapes=[pltpu.VMEM((D_MODEL, D_FF), BF16),
                        pltpu.VMEM((D_MODEL, D_FF), BF16),
                        pltpu.VMEM((D_FF, D_MODEL), BF16),
                        pltpu.VMEM((2, FFN_LOAD_ROWS, D_FF), F32),
                        pltpu.VMEM((2, FFN_LOAD_ROWS, D_FF), F32),
                        pltpu.VMEM((1, FFN_CHUNK, D_MODEL), F32),
                        pltpu.SemaphoreType.DMA((3, 2))],
        compiler_params=_params(58, 1),
        name="ffn_mix" if mix is not None else "ffn",
    )(*operands)


K_SECTION = 2


def _inproj_kernel(x_ref, g_ref, w_ref, wkt_ref, *out_refs):
    h = (_rms_rows(x_ref[...]) * g_ref[...]).astype(BF16)
    off = 0
    for n, (o_ref, width) in enumerate(zip(out_refs, IN_SECTIONS)):
        if n == K_SECTION:
            o_ref[...] = lax.dot_general(wkt_ref[...], h, (((1,), (1,)), ((), ())),
                                         preferred_element_type=F32)
        else:
            o_ref[...] = _dot(h, w_ref[:, off:off + width])
        off += width


def _inproj(x, gain, w_in, w_k_t, layer):
    t = x.shape[0]
    row = lambda w: pl.BlockSpec((PROJ_ROWS, w), lambda i: (i, 0))
    out_specs = [row(w) for w in IN_SECTIONS]
    out_shape = [jax.ShapeDtypeStruct((t, w), F32) for w in IN_SECTIONS]
    out_specs[K_SECTION] = pl.BlockSpec((RET_WIDTH, PROJ_ROWS), lambda i: (0, i))
    out_shape[K_SECTION] = jax.ShapeDtypeStruct((RET_WIDTH, t), F32)
    return pl.pallas_call(
        _inproj_kernel,
        grid=(t // PROJ_ROWS,),
        in_specs=[row(D_MODEL),
                  _resident((1, D_MODEL), layer),
                  _resident((D_MODEL, IN_WIDTH), layer),
                  _resident((RET_WIDTH, D_MODEL), layer)],
        out_specs=out_specs,
        out_shape=out_shape,
        compiler_params=_params(52, 1),
        name="inproj",
    )(x, gain, w_in, w_k_t)


S5_CHUNK = 8
S5_CAT = S5_CHUNK * S5_WIDTH
S5_CONST_ROWS = 8 * SUBLANES


def _cmul(a, b):
    return a[0] * b[0] - a[1] * b[1], a[0] * b[1] + a[1] * b[0]


def _s5_prep_kernel(lr_ref, li_ref, ls_ref, er_ref, ei_ref, ctr_ref, cti_ref, cst_ref,
                    wst_ref, kk_ref, vt_ref, c_ref):
    k = pl.program_id(1)
    lr = lr_ref[...]
    li = li_ref[...]
    step = jnp.exp(ls_ref[...])
    ar = lr * step
    ai = li * step
    mag = jnp.exp(ar)
    p1 = (mag * jnp.cos(ai), mag * jnp.sin(ai))
    nr = p1[0] - 1.0
    den = lr * lr + li * li
    f = ((nr * lr + p1[1] * li) / den, (p1[1] * lr - nr * li) / den)
    powers = [(jnp.ones_like(lr), jnp.zeros_like(lr)), p1]
    for _ in range(2, S5_CHUNK + 1):
        powers.append(_cmul(powers[-1], p1))

    def pick(n):
        out = powers[0]
        for i in range(1, S5_CHUNK + 1):
            out = (jnp.where(n == i, powers[i][0], out[0]), jnp.where(n == i, powers[i][1], out[1]))
        return out

    fp = _cmul(f, pick(k))
    er = er_ref[...]
    ei = ei_ref[...]
    w = jnp.concatenate([fp[0] * er - fp[1] * ei, fp[0] * ei + fp[1] * er], axis=1)
    wb = w.astype(BF16)
    wst_ref[...] = wb
    kk_ref[...] = _dot(wb, cst_ref[...]).astype(BF16)
    a, b = pick(k + 1)
    ctr = ctr_ref[...]
    cti = cti_ref[...]
    vt_ref[:, :S5_CH] = (ctr * a - cti * b).astype(BF16)
    vt_ref[:, S5_CH:] = (-(ctr * b) - cti * a).astype(BF16)

    @pl.when(k == 0)
    def _():
        q = [powers[S5_CHUNK]]
        for _ in range(1, SUBLANES):
            q.append(_cmul(q[-1], q[0]))
        row = lax.broadcasted_iota(jnp.int32, (SUBLANES, S5_CH), 0)
        zero = jnp.zeros((SUBLANES, S5_CH), F32)
        for s, d in enumerate((1, 2, 4)):
            for part in range(2):
                base = (2 * s + part) * SUBLANES
                c_ref[base:base + SUBLANES, :] = jnp.where(
                    row >= d, jnp.broadcast_to(q[d - 1][part], (SUBLANES, S5_CH)), zero)
        for part in range(2):
            tile = zero
            for r in range(SUBLANES):
                tile = jnp.where(row == r, jnp.broadcast_to(q[r][part], (SUBLANES, S5_CH)), tile)
            base = (6 + part) * SUBLANES
            c_ref[base:base + SUBLANES, :] = tile


def _s5_prepare(lam_re, lam_im, log_step, b_re, b_im, c_re, c_im):
    depth = lam_re.shape[0]
    def embed(a):
        n, m = a.shape[2:]
        same = np.equal.outer(np.repeat(np.arange(S5_GROUPS), n), np.arange(S5_GROUPS))
        tiled = jnp.broadcast_to(a.reshape(depth, S5_GROUPS * n, 1, m),
                                 (depth, S5_GROUPS * n, S5_GROUPS, m))
        return jnp.where(same[None, :, :, None], tiled, 0.0).reshape(
            depth, S5_GROUPS * n, S5_GROUPS * m)

    b_rows = lambda b: embed(jnp.swapaxes(b, 2, 3))
    c_rows = embed
    c_cols = lambda c: embed(jnp.swapaxes(c, 2, 3))
    c_stack = jnp.concatenate([c_cols(c_re), -c_cols(c_im)], axis=1).astype(BF16)
    vec = lambda a: a.reshape(depth, 1, S5_CH)
    ls = jnp.broadcast_to(log_step[:, :, None], (depth, S5_GROUPS, S5_STATE))
    vspec = pl.BlockSpec((None, 1, S5_CH), lambda l, k: (l, 0, 0))
    mspec = pl.BlockSpec((None, S5_WIDTH, S5_CH), lambda l, k: (l, 0, 0))
    reverse = lambda l, k: (l, S5_CHUNK - 1 - k, 0)
    return pl.pallas_call(
        _s5_prep_kernel,
        grid=(depth, S5_CHUNK),
        in_specs=[vspec, vspec, vspec, mspec, mspec, mspec, mspec,
                  pl.BlockSpec((None, 2 * S5_CH, S5_WIDTH), lambda l, k: (l, 0, 0))],
        out_specs=[pl.BlockSpec((None, S5_WIDTH, 2 * S5_CH), reverse),
                   pl.BlockSpec((None, S5_WIDTH, S5_WIDTH), reverse),
                   pl.BlockSpec((None, S5_WIDTH, 2 * S5_CH), lambda l, k: (l, k, 0)),
                   pl.BlockSpec((None, S5_CONST_ROWS, S5_CH), lambda l, k: (l, 0, 0))],
        out_shape=[jax.ShapeDtypeStruct((depth, S5_CAT, 2 * S5_CH), BF16),
                   jax.ShapeDtypeStruct((depth, S5_CAT, S5_WIDTH), BF16),
                   jax.ShapeDtypeStruct((depth, S5_CAT, 2 * S5_CH), BF16),
                   jax.ShapeDtypeStruct((depth, S5_CONST_ROWS, S5_CH), F32)],
        compiler_params=_params(40, 2),
        name="s5_prep",
    )(vec(lam_re), vec(lam_im), vec(ls), b_rows(b_re), b_rows(b_im), c_rows(c_re), c_rows(c_im),
      c_stack)


def _s5_state_kernel(ulo_ref, uhi_ref, wst_ref, k_ref, uc_ref, sp_ref, ds_scr, s_scr):
    j = pl.program_id(1)
    nch = ulo_ref.shape[0] // S5_CHUNK
    ncol = S5_CH // LANES
    cols = [(slice(cb * LANES, (cb + 1) * LANES),
             slice(S5_CH + cb * LANES, S5_CH + (cb + 1) * LANES)) for cb in range(ncol)]

    @pl.when(j == 0)
    def _():
        s_scr[0:SUBLANES, :] = jnp.zeros((SUBLANES, 2 * S5_CH), F32)

    @pl.when(j > 0)
    def _():
        s_scr[0:SUBLANES, :] = s_scr[nch:nch + SUBLANES, :]

    uc = jnp.concatenate([half[pl.ds(jj, nch, stride=S5_CHUNK), :]
                          for jj in range(S5_CHUNK) for half in (ulo_ref, uhi_ref)],
                         axis=1).astype(BF16)
    uc_ref[...] = uc
    ds_scr[...] = _dot(uc, wst_ref[...])

    def tile_body(i, carry):
        off = pl.multiple_of(i * SUBLANES, SUBLANES)
        loaded = [(ds_scr[pl.ds(off, SUBLANES), re], ds_scr[pl.ds(off, SUBLANES), im])
                  for re, im in cols]
        done = []
        for cb, (br, bi) in enumerate(loaded):
            re = cols[cb][0]
            for s, d in enumerate((1, 2, 4)):
                cr = k_ref[(2 * s) * SUBLANES:(2 * s + 1) * SUBLANES, re]
                ci = k_ref[(2 * s + 1) * SUBLANES:(2 * s + 2) * SUBLANES, re]
                rr = pltpu.roll(br, d, 0)
                ri = pltpu.roll(bi, d, 0)
                br, bi = br + (cr * rr - ci * ri), bi + (cr * ri + ci * rr)
            pr, pi = carry[cb], carry[ncol + cb]
            wr = k_ref[6 * SUBLANES:7 * SUBLANES, re]
            wi = k_ref[7 * SUBLANES:8 * SUBLANES, re]
            done.append((br + (wr * pr - wi * pi), bi + (wr * pi + wi * pr)))
        for (re, im), (sr, si) in zip(cols, done):
            s_scr[pl.ds(off + SUBLANES, SUBLANES), re] = sr
            s_scr[pl.ds(off + SUBLANES, SUBLANES), im] = si
        return tuple(last(sr) for sr, _ in done) + tuple(last(si) for _, si in done)

    last = lambda a: jnp.broadcast_to(a[SUBLANES - 1:SUBLANES, :], (SUBLANES, LANES))
    init = (tuple(last(s_scr[0:SUBLANES, re]) for re, _ in cols)
            + tuple(last(s_scr[0:SUBLANES, im]) for _, im in cols))
    lax.fori_loop(0, nch // SUBLANES, tile_body, init)

    sp_ref[...] = s_scr[SUBLANES - 1:SUBLANES - 1 + nch, :].astype(BF16)


def _s5_state(u, wst, consts, layer, batch):
    t = u.shape[0]
    nblk = t // batch // S5_STATE_ROWS
    nch = S5_STATE_ROWS // S5_CHUNK
    chunk_rows = pl.BlockSpec((nch, S5_CAT), lambda b, j: (b * nblk + j, 0))
    return pl.pallas_call(
        _s5_state_kernel,
        grid=(batch, nblk),
        in_specs=[pl.BlockSpec((S5_STATE_ROWS, LANES), lambda b, j: (b * nblk + j, 0)),
                  pl.BlockSpec((S5_STATE_ROWS, LANES), lambda b, j: (b * nblk + j, 1)),
                  _resident((S5_CAT, 2 * S5_CH), layer),
                  _resident((S5_CONST_ROWS, S5_CH), layer)],
        out_specs=[chunk_rows, chunk_rows],
        out_shape=[jax.ShapeDtypeStruct((t // S5_CHUNK, S5_CAT), BF16),
                   jax.ShapeDtypeStruct((t // S5_CHUNK, 2 * S5_CH), BF16)],
        scratch_shapes=[pltpu.VMEM((nch, 2 * S5_CH), F32),
                        pltpu.VMEM((nch + SUBLANES, 2 * S5_CH), F32)],
        compiler_params=_params(44, 2),
        name="s5_state",
    )(u, u, wst, consts)


def _s5_out_kernel(u_ref, uc_ref, sp_ref, krev_ref, vt_ref, d_ref, wglu_ref, bglu_ref, gain_ref,
                   o_ref, ylo_scr, yhi_scr):
    nch = uc_ref.shape[0]
    sp = sp_ref[...]
    for r in range(S5_CHUNK):
        cols = slice(r * S5_WIDTH, (r + 1) * S5_WIDTH)
        used = (r + 1) * S5_WIDTH
        y_r = _dot(uc_ref[:, :used], krev_ref[S5_CAT - used:, :])
        y_r = y_r + lax.dot_general(sp, vt_ref[cols, :], (((1,), (1,)), ((), ())),
                                    preferred_element_type=F32)
        ylo_scr[pl.ds(r, nch, stride=S5_CHUNK), :] = y_r[:, :LANES]
        yhi_scr[pl.ds(r, nch, stride=S5_CHUNK), :] = y_r[:, LANES:]
    y = jnp.concatenate([ylo_scr[...], yhi_scr[...]], axis=1) + d_ref[...] * u_ref[...]
    y = _gelu_tanh(y)
    y = y * _sigmoid(_dot(y.astype(BF16), wglu_ref[...]) + bglu_ref[...])
    o_ref[...] = (_rms_rows(y) * gain_ref[...]).astype(BF16)


def _s5_out(u, uc, sp, krev, vt, d_skip, w_glu, b_glu, gain, layer):
    t = u.shape[0]
    nch = S5_OUT_ROWS // S5_CHUNK
    row = pl.BlockSpec((S5_OUT_ROWS, S5_WIDTH), lambda i: (i, 0))
    chunk_rows = pl.BlockSpec((nch, S5_CAT), lambda i: (i, 0))
    return pl.pallas_call(
        _s5_out_kernel,
        grid=(t // S5_OUT_ROWS,),
        in_specs=[row, chunk_rows, chunk_rows,
                  _resident((S5_CAT, S5_WIDTH), layer),
                  _resident((S5_CAT, 2 * S5_CH), layer),
                  _resident((1, S5_WIDTH), layer),
                  _resident((S5_WIDTH, S5_WIDTH), layer),
                  _resident((1, S5_WIDTH), layer),
                  _resident((1, S5_WIDTH), layer)],
        out_specs=row,
        out_shape=jax.ShapeDtypeStruct((t, S5_WIDTH), BF16),
        scratch_shapes=[pltpu.VMEM((S5_OUT_ROWS, LANES), F32),
                        pltpu.VMEM((S5_OUT_ROWS, LANES), F32)],
        compiler_params=_params(48, 1),
        name="s5_out",
    )(u, uc, sp, krev, vt, d_skip, w_glu, b_glu, gain)


def _lru_kernel(x_ref, gate_ref, cw_ref, cb_ref, wax_ref, bax_ref, lam_ref, gain_ref, o_ref,
                ext_scr, a_scr, b_scr, h_scr, carry_scr):
    j = pl.program_id(1)
    rows = x_ref.shape[0]
    pad = SUBLANES

    @pl.when(j == 0)
    def _():
        ext_scr[0:pad, :] = jnp.zeros((pad, LRU_WIDTH), F32)
        carry_scr[...] = jnp.zeros(carry_scr.shape, F32)

    @pl.when(j > 0)
    def _():
        ext_scr[0:pad, :] = ext_scr[rows:rows + pad, :]

    ext_scr[pad:rows + pad, :] = x_ref[...]
    xc = cb_ref[...] + cw_ref[CONV_WIDTH - 1:CONV_WIDTH, :] * ext_scr[pad:rows + pad, :]
    for lag in range(1, CONV_WIDTH):
        tap = cw_ref[CONV_WIDTH - 1 - lag:CONV_WIDTH - lag, :]
        xc = xc + tap * ext_scr[pad - lag:rows + pad - lag, :]

    gates = _sigmoid(_dot(xc.astype(BF16), wax_ref[...]) + bax_ref[...])
    r = gates[:, :LRU_WIDTH]
    i = gates[:, LRU_WIDTH:]
    z = -lam_ref[...]
    softplus = jnp.maximum(z, 0.0) + jnp.log1p(jnp.exp(-jnp.abs(z)))
    a = jnp.exp((-LRU_C) * r * softplus)
    a_scr[...] = a
    b_scr[...] = jnp.sqrt(1.0 - a * a) * (i * xc)

    row = lax.broadcasted_iota(jnp.int32, (SUBLANES, LRU_WIDTH), 0)

    def tile_body(t, prev):
        off = pl.multiple_of(t * SUBLANES, SUBLANES)
        at = a_scr[pl.ds(off, SUBLANES), :]
        bt = b_scr[pl.ds(off, SUBLANES), :]
        for d in (1, 2, 4):
            keep = row >= d
            ar = jnp.where(keep, pltpu.roll(at, d, 0), 1.0)
            br = jnp.where(keep, pltpu.roll(bt, d, 0), 0.0)
            bt = at * br + bt
            at = at * ar
        h = at * prev + bt
        h_scr[pl.ds(off, SUBLANES), :] = h
        return jnp.broadcast_to(h[SUBLANES - 1:SUBLANES, :], (SUBLANES, LRU_WIDTH))

    carry_scr[...] = lax.fori_loop(0, rows // SUBLANES, tile_body, carry_scr[...],
                                   unroll=LRU_UNROLL)

    y = h_scr[...] * _gelu_tanh(gate_ref[...])
    o_ref[...] = (_rms_rows(y) * gain_ref[...]).astype(BF16)


def _lru(x, gate, conv_w, conv_b, wax, bax, lam, gain, layer, batch):
    t = x.shape[0]
    nblk = t // batch // LRU_ROWS
    row = pl.BlockSpec((LRU_ROWS, LRU_WIDTH), lambda b, j: (b * nblk + j, 0))
    return pl.pallas_call(
        _lru_kernel,
        grid=(batch, nblk),
        in_specs=[row, row,
                  _resident((CONV_WIDTH, LRU_WIDTH), layer),
                  _resident((1, LRU_WIDTH), layer),
                  _resident((LRU_WIDTH, 2 * LRU_WIDTH), layer),
                  _resident((1, 2 * LRU_WIDTH), layer),
                  _resident((1, LRU_WIDTH), layer),
                  _resident((1, LRU_WIDTH), layer)],
        out_specs=row,
        out_shape=jax.ShapeDtypeStruct((t, LRU_WIDTH), BF16),
        scratch_shapes=[pltpu.VMEM((LRU_ROWS + SUBLANES, LRU_WIDTH), F32),
                        pltpu.VMEM((LRU_ROWS, LRU_WIDTH), F32),
                        pltpu.VMEM((LRU_ROWS, LRU_WIDTH), F32),
                        pltpu.VMEM((LRU_ROWS, LRU_WIDTH), F32),
                        pltpu.VMEM((SUBLANES, LRU_WIDTH), F32)],
        compiler_params=_params(48, 2),
        name="lru",
    )(x, gate, conv_w, conv_b, wax, bax, lam, gain)


def _ret_kernel(q_ref, kt_ref, v_ref, g_ref, cos_ref, sin_ref, cost_ref, sint_ref, decay_ref,
                zetat_ref, xi_ref, gamma_ref, gain_ref, o_ref, state_scr, y_scr):
    j = pl.program_id(1)

    @pl.when(j == 0)
    def _():
        state_scr[...] = jnp.zeros(state_scr.shape, F32)

    half = RET_HEAD_DIM // 2

    def chunk(start):
        rows = pl.ds(start, RET_CHUNK)
        cs = cos_ref[rows, :]
        sn = sin_ref[rows, :]
        cst = cost_ref[:, rows]
        snt = sint_ref[:, rows]
        ssq = jnp.zeros((RET_CHUNK, 1), F32)
        for h in range(RET_HEADS):
            cols = slice(h * RET_HEAD_DIM, (h + 1) * RET_HEAD_DIM)
            qh = q_ref[rows, cols]
            kt = kt_ref[cols, rows]
            vh = v_ref[rows, cols].astype(BF16)
            qb = (qh * cs + pltpu.roll(qh, half, 1) * sn).astype(BF16)
            kr = kt * cst + jnp.concatenate([kt[half:], kt[:half]], axis=0) * snt
            scores = _dot(qb, kr.astype(BF16)) * decay_ref[:, cols]
            inner = _dot(scores.astype(BF16), vh)
            st = state_scr[:, cols]
            cross = _dot(qb, st.astype(BF16)) * xi_ref[:, cols]
            kz = (kr * zetat_ref[cols, :]).astype(BF16)
            state_scr[:, cols] = gamma_ref[:, cols] * st + _dot(kz, vh)
            yh = _silu(g_ref[rows, cols]) * _rms_rows(inner + cross)
            ssq = ssq + jnp.sum(yh * yh, axis=-1, keepdims=True)
            y_scr[:, cols] = yh
        inv = lax.rsqrt(ssq * (1.0 / RET_WIDTH) + NORM_EPS)
        o_ref[rows, :] = ((y_scr[...] * inv) * gain_ref[...]).astype(BF16)

    def body(i, carry):
        for c in range(RET_UNROLL):
            chunk(pl.multiple_of((i * RET_UNROLL + c) * RET_CHUNK, RET_CHUNK))
        return carry

    lax.fori_loop(0, q_ref.shape[0] // (RET_CHUNK * RET_UNROLL), body, 0)


def _ret_tables(seq):
    dh, c, nh = RET_HEAD_DIM, RET_CHUNK, RET_HEADS
    pos = np.arange(seq, dtype=np.float64)
    inv_freq = ROPE_BASE ** (-np.arange(0, dh, 2, dtype=np.float64) / dh)
    ang = pos[:, None] * inv_freq[None, :]
    cos = np.cos(ang)
    sin = np.sin(ang)
    cos2 = np.concatenate([cos, cos], axis=-1)
    sin2 = np.concatenate([-sin, sin], axis=-1)
    log_gamma = np.log1p(-np.exp2(-5.0 - np.arange(nh, dtype=np.float64)))
    idx = np.arange(c, dtype=np.float64)
    diff = idx[:, None] - idx[None, :]
    decay = np.where(diff[None] >= 0,
                     np.exp(np.maximum(diff, 0.0)[None] * log_gamma[:, None, None]), 0.0)
    zeta = np.exp((c - 1.0 - idx)[None] * log_gamma[:, None])
    xi = np.exp((idx + 1.0)[None] * log_gamma[:, None])
    gamma_chunk = np.exp(c * log_gamma)
    decay_l = decay.transpose(1, 0, 2).reshape(c, nh * c)
    zeta_t = np.broadcast_to(zeta[:, None, :], (nh, dh, c)).reshape(nh * dh, c)
    xi_l = np.broadcast_to(xi.T[:, :, None], (c, nh, dh)).reshape(c, nh * dh)
    gamma_l = np.broadcast_to(gamma_chunk[:, None], (nh, dh)).reshape(1, nh * dh)
    scale = dh ** -0.5
    tables = (cos2 * scale, sin2 * scale, cos2.T, sin2.T, decay_l, zeta_t, xi_l, gamma_l)
    return tuple(jnp.asarray(np.ascontiguousarray(a, dtype=np.float32)) for a in tables)


def _ret(q, k_t, v, g, tables, gain, layer, batch):
    t = q.shape[0]
    nblk = t // batch // RET_ROWS
    row = pl.BlockSpec((RET_ROWS, RET_WIDTH), lambda b, j: (b * nblk + j, 0))
    row_t = pl.BlockSpec((RET_WIDTH, RET_ROWS), lambda b, j: (0, b * nblk + j))
    rot = pl.BlockSpec((RET_ROWS, RET_HEAD_DIM), lambda b, j: (j, 0))
    rot_t = pl.BlockSpec((RET_HEAD_DIM, RET_ROWS), lambda b, j: (0, j))
    const = lambda shape: pl.BlockSpec(shape, lambda b, j: (0, 0), pipeline_mode=pl.Buffered(1))
    return pl.pallas_call(
        _ret_kernel,
        grid=(batch, nblk),
        in_specs=[row, row_t, row, row, rot, rot, rot_t, rot_t,
                  const((RET_CHUNK, RET_WIDTH)), const((RET_WIDTH, RET_CHUNK)),
                  const((RET_CHUNK, RET_WIDTH)), const((1, RET_WIDTH)),
                  _resident((1, RET_WIDTH), layer)],
        out_specs=row,
        out_shape=jax.ShapeDtypeStruct((t, RET_WIDTH), BF16),
        scratch_shapes=[pltpu.VMEM((RET_HEAD_DIM, RET_WIDTH), F32),
                        pltpu.VMEM((RET_CHUNK, RET_WIDTH), F32)],
        compiler_params=_params(32, 2),
        name="ret",
    )(q, k_t, v, g, *tables, gain)


def _block_diag(w):
    depth, n, d, e = w.shape
    eye = jnp.eye(n, dtype=w.dtype)
    return jnp.einsum('lnde,nm->lndme', w, eye).reshape(depth, n * d, n * e)


def kernel(x, ffn1_norm, ffn1_w_gate, ffn1_w_up, ffn1_w_down, mix_norm, w_in, s5_lambda_re, s5_lambda_im, s5_log_step, s5_b_re, s5_b_im, s5_c_re, s5_c_im, s5_d, s5_w_glu, s5_b_glu, s5_out_norm, ret_out_norm, lru_conv_w, lru_conv_b, lru_w_a, lru_b_a, lru_w_x, lru_b_x, lru_lambda, lru_out_norm, w_out, ffn2_norm, ffn2_w_gate, ffn2_w_up, ffn2_w_down, final_norm):
    batch, seq, d = x.shape
    depth = w_in.shape[0]
    t = batch * seq
    assert d == D_MODEL and seq % max(S5_STATE_ROWS, LRU_ROWS, RET_ROWS) == 0
    assert t % max(FFN_ROWS, PROJ_ROWS, S5_OUT_ROWS) == 0

    row3 = lambda a: a.reshape(depth, 1, a.shape[-1])
    bf = lambda a: a.astype(BF16)

    f1 = (row3(ffn1_norm), ffn1_w_gate, ffn1_w_up, ffn1_w_down)
    f2 = (row3(ffn2_norm), ffn2_w_gate, ffn2_w_up, ffn2_w_down)
    w_in_b = bf(w_in)
    k_off = sum(IN_SECTIONS[:K_SECTION])
    w_k = lax.optimization_barrier(w_in[:, :, k_off:k_off + RET_WIDTH])
    w_k_t = bf(jnp.swapaxes(w_k, 1, 2))
    w_out_b = bf(w_out)
    s5_wst, s5_krev, s5_vt, s5_consts = _s5_prepare(s5_lambda_re, s5_lambda_im, s5_log_step,
                                                 s5_b_re, s5_b_im, s5_c_re, s5_c_im)
    s5_glu = bf(s5_w_glu)
    lru_wax = bf(jnp.concatenate([_block_diag(lru_w_a), _block_diag(lru_w_x)], axis=-1))
    lru_bax = row3(jnp.concatenate([lru_b_a, lru_b_x], axis=-1))
    tables = _ret_tables(seq)

    xt = x.reshape(t, d)
    for l in range(depth):
        xt = _ffn(xt, *f1, l)
        u, q, k_t, v, g, xl, gl = _inproj(xt, row3(mix_norm), w_in_b, w_k_t, l)
        uc, sp = _s5_state(u, s5_wst, s5_consts, l, batch)
        y_s5 = _s5_out(u, uc, sp, s5_krev, s5_vt, row3(s5_d), s5_glu, row3(s5_b_glu),
                       row3(s5_out_norm), l)
        y_ret = _ret(q, k_t, v, g, tables, row3(ret_out_norm), l, batch)
        y_lru = _lru(xl, gl, lru_conv_w, row3(lru_conv_b), lru_wax, lru_bax, row3(lru_lambda),
                     row3(lru_out_norm), l, batch)
        last = final_norm.reshape(1, d) if l == depth - 1 else None
        xt = _ffn(xt, *f2, l, mix=(y_s5, y_ret, y_lru, w_out_b), final_gain=last)
    return xt.reshape(batch, seq, d)
```

```python
import functools
import math

import jax
import jax.numpy as jnp
import numpy as np
from jax import lax
from jax.experimental import pallas as pl
from jax.experimental.pallas import tpu as pltpu

F32 = jnp.float32
BF16 = jnp.bfloat16

D_MODEL = 1024
D_FF = 2816
S5_WIDTH = 256
S5_GROUP_DIM = 16
S5_GROUPS = 16
S5_STATE = 64
S5_CH = S5_GROUPS * S5_STATE
RET_WIDTH = 512
RET_HEAD_DIM = 128
RET_HEADS = 4
RET_CHUNK = 128
LRU_WIDTH = 256
LRU_BLOCKS = 4
LRU_BLOCK_DIM = 64
CONV_WIDTH = 4
LRU_C = 8.0
ROPE_BASE = 10000.0
NORM_EPS = 1e-6
IN_SECTIONS = (S5_WIDTH, RET_WIDTH, RET_WIDTH, RET_WIDTH, RET_WIDTH, LRU_WIDTH, LRU_WIDTH)
IN_WIDTH = sum(IN_SECTIONS)

SUBLANES = 8
LANES = 128
MIB = 1024 * 1024

FFN_ROWS = 1024
FFN_CHUNK = 256
PROJ_ROWS = 1024
S5_STATE_ROWS = 4096
S5_OUT_ROWS = 4096
LRU_ROWS = 2048
LRU_UNROLL = 4
RET_ROWS = 1024
RET_UNROLL = 2


def _params(vmem_mib, n_axes):
    return pltpu.CompilerParams(
        dimension_semantics=("arbitrary",) * n_axes,
        vmem_limit_bytes=vmem_mib * MIB)


def _rms_rows(x):
    return x * lax.rsqrt(jnp.mean(x * x, axis=-1, keepdims=True) + NORM_EPS)


def _gelu_tanh(x):
    c = math.sqrt(2.0 / math.pi)
    return x * (0.5 * (1.0 + jnp.tanh(c * (x + 0.044715 * (x * x * x)))))


def _sigmoid(x):
    return 0.5 + 0.5 * jnp.tanh(0.5 * x)


def _silu(x):
    h = 0.5 * x
    return h + h * jnp.tanh(h)


def _dot(a, b):
    return jnp.dot(a, b, preferred_element_type=F32)


def _resident(shape, layer):
    nd = len(shape)
    return pl.BlockSpec((None,) + tuple(shape), lambda *_: (layer,) + (0,) * nd,
                        pipeline_mode=pl.Buffered(1))


def _ffn_kernel(*refs, with_outproj, with_final_norm):
    refs = list(refs)
    o_ref = refs.pop()
    x = refs.pop(0)[...]
    if with_outproj:
        ys_ref, yr_ref, yl_ref, wo_ref = refs[:4]
        refs = refs[4:]
        x = x + _dot(ys_ref[...], wo_ref[0:S5_WIDTH, :])
        x = x + _dot(yr_ref[...], wo_ref[S5_WIDTH:S5_WIDTH + RET_WIDTH, :])
        x = x + _dot(yl_ref[...], wo_ref[S5_WIDTH + RET_WIDTH:, :])
    g_ref, wg_ref, wu_ref, wd_ref = refs[:4]
    h = (_rms_rows(x) * g_ref[...]).astype(BF16)
    acc = jnp.zeros(x.shape, F32)
    for c in range(D_FF // FFN_CHUNK):
        sl = slice(c * FFN_CHUNK, (c + 1) * FFN_CHUNK)
        act = (_silu(_dot(h, wg_ref[:, sl])) * _dot(h, wu_ref[:, sl])).astype(BF16)
        acc = acc + _dot(act, wd_ref[sl, :])
    y = x + 0.5 * acc
    if with_final_norm:
        y = _rms_rows(y) * refs[4][...]
    o_ref[...] = y


def _ffn(x, gain, weights, layer, mix=None, final_gain=None):
    t = x.shape[0]
    spec = lambda w: pl.BlockSpec((FFN_ROWS, w), lambda i: (i, 0))
    whole = lambda a: pl.BlockSpec(a.shape, lambda i: (0, 0), pipeline_mode=pl.Buffered(1))
    operands = [x]
    in_specs = [spec(D_MODEL)]
    if mix is not None:
        y_s5, y_ret, y_lru, w_out = mix
        operands += [y_s5, y_ret, y_lru, w_out]
        in_specs += [spec(S5_WIDTH), spec(RET_WIDTH), spec(LRU_WIDTH),
                     _resident((D_MODEL, D_MODEL), layer)]
    operands += [gain, *weights]
    in_specs += [_resident((1, D_MODEL), layer)] + [whole(w) for w in weights]
    if final_gain is not None:
        operands.append(final_gain)
        in_specs.append(pl.BlockSpec((1, D_MODEL), lambda i: (0, 0)))
    return pl.pallas_call(
        functools.partial(_ffn_kernel, with_outproj=mix is not None,
                          with_final_norm=final_gain is not None),
        grid=(t // FFN_ROWS,),
        in_specs=in_specs,
        out_specs=spec(D_MODEL),
        out_shape=jax.ShapeDtypeStruct(x.shape, F32),
        compiler_params=_params(56 if mix is not None else 50, 1),
        name="ffn_mix" if mix is not None else "ffn",
    )(*operands)


FFN_WEIGHT_SHAPES = ((D_MODEL, D_FF), (D_MODEL, D_FF), (D_FF, D_MODEL))


def _cast_plan(nsteps):
    slabs = [(r // nsteps, c) for r, c in FFN_WEIGHT_SHAPES]
    hbm = pl.BlockSpec(memory_space=pl.ANY)
    scratch = ([pltpu.VMEM((2,) + sl, F32) for sl in slabs] + [pltpu.VMEM(sl, BF16) for sl in slabs]
               + [pltpu.SemaphoreType.DMA((3, 3))])
    out_shape = [jax.ShapeDtypeStruct(shape, BF16) for shape in FFN_WEIGHT_SHAPES]
    return [hbm] * 3, [hbm] * 3, out_shape, scratch


def _cast_copies(layer, srcs, dsts, scratch):
    stage_in, stage_out, sem = scratch[:3], scratch[3:6], scratch[6]

    def rows(i, s):
        n = stage_out[i].shape[0]
        return pl.ds(pl.multiple_of(s * n, 16), n)

    fetch = lambda i, s: pltpu.make_async_copy(srcs[i].at[layer, rows(i, s), :],
                                               stage_in[i].at[s % 2], sem.at[i, s % 2])
    put = lambda i, s: pltpu.make_async_copy(stage_out[i], dsts[i].at[rows(i, s), :], sem.at[i, 2])
    return fetch, put, stage_in, stage_out


def _cast_begin(step, nsteps, layer, srcs, dsts, scratch):
    fetch, put, stage_in, stage_out = _cast_copies(layer, srcs, dsts, scratch)

    @pl.when(step == 0)
    def _():
        for i in range(3):
            fetch(i, 0).start()

    for i in range(3):
        fetch(i, step).wait()
        stage_out[i][...] = stage_in[i][step % 2].astype(BF16)
        put(i, step).start()

    @pl.when(step + 1 < nsteps)
    def _():
        for i in range(3):
            fetch(i, step + 1).start()


def _cast_end(step, layer, srcs, dsts, scratch):
    _, put, _, _ = _cast_copies(layer, srcs, dsts, scratch)
    for i in range(3):
        put(i, step).wait()


K_SECTION = 2


def _inproj_kernel(x_ref, g_ref, w_ref, wkt_ref, *refs, layer, nsteps):
    srcs, refs = refs[:3], refs[3:]
    out_refs, dsts, scratch = refs[:len(IN_SECTIONS)], refs[len(IN_SECTIONS):-7][:3], refs[-7:]
    step = pl.program_id(0)
    _cast_begin(step, nsteps, layer, srcs, dsts, scratch)

    h = (_rms_rows(x_ref[...]) * g_ref[...]).astype(BF16)
    off = 0
    for n, (o_ref, width) in enumerate(zip(out_refs, IN_SECTIONS)):
        if n == K_SECTION:
            o_ref[...] = lax.dot_general(wkt_ref[...], h, (((1,), (1,)), ((), ())),
                                         preferred_element_type=F32)
        else:
            o_ref[...] = _dot(h, w_ref[:, off:off + width])
        off += width

    _cast_end(step, layer, srcs, dsts, scratch)


def _inproj(x, gain, w_in, w_k_t, ffn_f32, layer):
    t = x.shape[0]
    nsteps = t // PROJ_ROWS
    row = lambda w: pl.BlockSpec((PROJ_ROWS, w), lambda i: (i, 0))
    out_specs = [row(w) for w in IN_SECTIONS]
    out_shape = [jax.ShapeDtypeStruct((t, w), F32) for w in IN_SECTIONS]
    out_specs[K_SECTION] = pl.BlockSpec((RET_WIDTH, PROJ_ROWS), lambda i: (0, i))
    out_shape[K_SECTION] = jax.ShapeDtypeStruct((RET_WIDTH, t), F32)
    cast_in, cast_out, cast_shape, cast_scratch = _cast_plan(nsteps)
    outs = pl.pallas_call(
        functools.partial(_inproj_kernel, layer=layer, nsteps=nsteps),
        grid=(nsteps,),
        in_specs=[row(D_MODEL),
                  _resident((1, D_MODEL), layer),
                  _resident((D_MODEL, IN_WIDTH), layer),
                  _resident((RET_WIDTH, D_MODEL), layer)] + cast_in,
        out_specs=out_specs + cast_out,
        out_shape=out_shape + cast_shape,
        scratch_shapes=cast_scratch,
        compiler_params=_params(56, 1),
        name="inproj",
    )(x, gain, w_in, w_k_t, *ffn_f32)
    return outs[:len(IN_SECTIONS)], tuple(outs[len(IN_SECTIONS):])


S5_CHUNK = 8
S5_CAT = S5_CHUNK * S5_WIDTH
S5_CONST_ROWS = 8 * SUBLANES


def _cmul(a, b):
    return a[0] * b[0] - a[1] * b[1], a[0] * b[1] + a[1] * b[0]


def _s5_prep_kernel(lr_ref, li_ref, ls_ref, er_ref, ei_ref, ctr_ref, cti_ref, cst_ref,
                    wst_ref, kk_ref, vt_ref, c_ref):
    k = pl.program_id(1)
    lr = lr_ref[...]
    li = li_ref[...]
    step = jnp.exp(ls_ref[...])
    ar = lr * step
    ai = li * step
    mag = jnp.exp(ar)
    p1 = (mag * jnp.cos(ai), mag * jnp.sin(ai))
    nr = p1[0] - 1.0
    den = lr * lr + li * li
    f = ((nr * lr + p1[1] * li) / den, (p1[1] * lr - nr * li) / den)
    powers = [(jnp.ones_like(lr), jnp.zeros_like(lr)), p1]
    for _ in range(2, S5_CHUNK + 1):
        powers.append(_cmul(powers[-1], p1))

    def pick(n):
        out = powers[0]
        for i in range(1, S5_CHUNK + 1):
            out = (jnp.where(n == i, powers[i][0], out[0]), jnp.where(n == i, powers[i][1], out[1]))
        return out

    fp = _cmul(f, pick(k))
    er = er_ref[...]
    ei = ei_ref[...]
    w = jnp.concatenate([fp[0] * er - fp[1] * ei, fp[0] * ei + fp[1] * er], axis=1)
    wb = w.astype(BF16)
    wst_ref[...] = wb
    kk_ref[...] = _dot(wb, cst_ref[...]).astype(BF16)
    a, b = pick(k + 1)
    ctr = ctr_ref[...]
    cti = cti_ref[...]
    vt_ref[:, :S5_CH] = (ctr * a - cti * b).astype(BF16)
    vt_ref[:, S5_CH:] = (-(ctr * b) - cti * a).astype(BF16)

    @pl.when(k == 0)
    def _():
        q = [powers[S5_CHUNK]]
        for _ in range(1, SUBLANES):
            q.append(_cmul(q[-1], q[0]))
        row = lax.broadcasted_iota(jnp.int32, (SUBLANES, S5_CH), 0)
        zero = jnp.zeros((SUBLANES, S5_CH), F32)
        for s, d in enumerate((1, 2, 4)):
            for part in range(2):
                base = (2 * s + part) * SUBLANES
                c_ref[base:base + SUBLANES, :] = jnp.where(
                    row >= d, jnp.broadcast_to(q[d - 1][part], (SUBLANES, S5_CH)), zero)
        for part in range(2):
            tile = zero
            for r in range(SUBLANES):
                tile = jnp.where(row == r, jnp.broadcast_to(q[r][part], (SUBLANES, S5_CH)), tile)
            base = (6 + part) * SUBLANES
            c_ref[base:base + SUBLANES, :] = tile


def _s5_prepare(lam_re, lam_im, log_step, b_re, b_im, c_re, c_im):
    depth = lam_re.shape[0]
    def embed(a):
        n, m = a.shape[2:]
        same = np.equal.outer(np.repeat(np.arange(S5_GROUPS), n), np.arange(S5_GROUPS))
        tiled = jnp.broadcast_to(a.reshape(depth, S5_GROUPS * n, 1, m),
                                 (depth, S5_GROUPS * n, S5_GROUPS, m))
        return jnp.where(same[None, :, :, None], tiled, 0.0).reshape(
            depth, S5_GROUPS * n, S5_GROUPS * m)

    b_rows = lambda b: embed(jnp.swapaxes(b, 2, 3))
    c_rows = embed
    c_cols = lambda c: embed(jnp.swapaxes(c, 2, 3))
    c_stack = jnp.concatenate([c_cols(c_re), -c_cols(c_im)], axis=1).astype(BF16)
    vec = lambda a: a.reshape(depth, 1, S5_CH)
    ls = jnp.broadcast_to(log_step[:, :, None], (depth, S5_GROUPS, S5_STATE))
    vspec = pl.BlockSpec((None, 1, S5_CH), lambda l, k: (l, 0, 0))
    mspec = pl.BlockSpec((None, S5_WIDTH, S5_CH), lambda l, k: (l, 0, 0))
    reverse = lambda l, k: (l, S5_CHUNK - 1 - k, 0)
    return pl.pallas_call(
        _s5_prep_kernel,
        grid=(depth, S5_CHUNK),
        in_specs=[vspec, vspec, vspec, mspec, mspec, mspec, mspec,
                  pl.BlockSpec((None, 2 * S5_CH, S5_WIDTH), lambda l, k: (l, 0, 0))],
        out_specs=[pl.BlockSpec((None, S5_WIDTH, 2 * S5_CH), reverse),
                   pl.BlockSpec((None, S5_WIDTH, S5_WIDTH), reverse),
                   pl.BlockSpec((None, S5_WIDTH, 2 * S5_CH), lambda l, k: (l, k, 0)),
                   pl.BlockSpec((None, S5_CONST_ROWS, S5_CH), lambda l, k: (l, 0, 0))],
        out_shape=[jax.ShapeDtypeStruct((depth, S5_CAT, 2 * S5_CH), BF16),
                   jax.ShapeDtypeStruct((depth, S5_CAT, S5_WIDTH), BF16),
                   jax.ShapeDtypeStruct((depth, S5_CAT, 2 * S5_CH), BF16),
                   jax.ShapeDtypeStruct((depth, S5_CONST_ROWS, S5_CH), F32)],
        compiler_params=_params(40, 2),
        name="s5_prep",
    )(vec(lam_re), vec(lam_im), vec(ls), b_rows(b_re), b_rows(b_im), c_rows(c_re), c_rows(c_im),
      c_stack)


def _s5_state_kernel(ulo_ref, uhi_ref, wst_ref, k_ref, uc_ref, sp_ref, ds_scr, s_scr):
    j = pl.program_id(1)
    nch = ulo_ref.shape[0] // S5_CHUNK
    ncol = S5_CH // LANES
    cols = [(slice(cb * LANES, (cb + 1) * LANES),
             slice(S5_CH + cb * LANES, S5_CH + (cb + 1) * LANES)) for cb in range(ncol)]

    @pl.when(j == 0)
    def _():
        s_scr[0:SUBLANES, :] = jnp.zeros((SUBLANES, 2 * S5_CH), F32)

    @pl.when(j > 0)
    def _():
        s_scr[0:SUBLANES, :] = s_scr[nch:nch + SUBLANES, :]

    uc = jnp.concatenate([half[pl.ds(jj, nch, stride=S5_CHUNK), :]
                          for jj in range(S5_CHUNK) for half in (ulo_ref, uhi_ref)],
                         axis=1).astype(BF16)
    uc_ref[...] = uc
    ds_scr[...] = _dot(uc, wst_ref[...])

    def tile_body(i, carry):
        off = pl.multiple_of(i * SUBLANES, SUBLANES)
        loaded = [(ds_scr[pl.ds(off, SUBLANES), re], ds_scr[pl.ds(off, SUBLANES), im])
                  for re, im in cols]
        done = []
        for cb, (br, bi) in enumerate(loaded):
            re = cols[cb][0]
            for s, d in enumerate((1, 2, 4)):
                cr = k_ref[(2 * s) * SUBLANES:(2 * s + 1) * SUBLANES, re]
                ci = k_ref[(2 * s + 1) * SUBLANES:(2 * s + 2) * SUBLANES, re]
                rr = pltpu.roll(br, d, 0)
                ri = pltpu.roll(bi, d, 0)
                br, bi = br + (cr * rr - ci * ri), bi + (cr * ri + ci * rr)
            pr, pi = carry[cb], carry[ncol + cb]
            wr = k_ref[6 * SUBLANES:7 * SUBLANES, re]
            wi = k_ref[7 * SUBLANES:8 * SUBLANES, re]
            done.append((br + (wr * pr - wi * pi), bi + (wr * pi + wi * pr)))
        for (re, im), (sr, si) in zip(cols, done):
            s_scr[pl.ds(off + SUBLANES, SUBLANES), re] = sr
            s_scr[pl.ds(off + SUBLANES, SUBLANES), im] = si
        return tuple(last(sr) for sr, _ in done) + tuple(last(si) for _, si in done)

    last = lambda a: jnp.broadcast_to(a[SUBLANES - 1:SUBLANES, :], (SUBLANES, LANES))
    init = (tuple(last(s_scr[0:SUBLANES, re]) for re, _ in cols)
            + tuple(last(s_scr[0:SUBLANES, im]) for _, im in cols))
    lax.fori_loop(0, nch // SUBLANES, tile_body, init)

    sp_ref[...] = s_scr[SUBLANES - 1:SUBLANES - 1 + nch, :].astype(BF16)


def _s5_state(u, wst, consts, layer, batch):
    t = u.shape[0]
    nblk = t // batch // S5_STATE_ROWS
    nch = S5_STATE_ROWS // S5_CHUNK
    chunk_rows = pl.BlockSpec((nch, S5_CAT), lambda b, j: (b * nblk + j, 0))
    return pl.pallas_call(
        _s5_state_kernel,
        grid=(batch, nblk),
        in_specs=[pl.BlockSpec((S5_STATE_ROWS, LANES), lambda b, j: (b * nblk + j, 0)),
                  pl.BlockSpec((S5_STATE_ROWS, LANES), lambda b, j: (b * nblk + j, 1)),
                  _resident((S5_CAT, 2 * S5_CH), layer),
                  _resident((S5_CONST_ROWS, S5_CH), layer)],
        out_specs=[chunk_rows, chunk_rows],
        out_shape=[jax.ShapeDtypeStruct((t // S5_CHUNK, S5_CAT), BF16),
                   jax.ShapeDtypeStruct((t // S5_CHUNK, 2 * S5_CH), BF16)],
        scratch_shapes=[pltpu.VMEM((nch, 2 * S5_CH), F32),
                        pltpu.VMEM((nch + SUBLANES, 2 * S5_CH), F32)],
        compiler_params=_params(44, 2),
        name="s5_state",
    )(u, u, wst, consts)


def _s5_out_kernel(u_ref, uc_ref, sp_ref, krev_ref, vt_ref, d_ref, wglu_ref, bglu_ref, gain_ref,
                   o_ref, ylo_scr, yhi_scr):
    nch = uc_ref.shape[0]
    sp = sp_ref[...]
    for r in range(S5_CHUNK):
        cols = slice(r * S5_WIDTH, (r + 1) * S5_WIDTH)
        used = (r + 1) * S5_WIDTH
        y_r = _dot(uc_ref[:, :used], krev_ref[S5_CAT - used:, :])
        y_r = y_r + lax.dot_general(sp, vt_ref[cols, :], (((1,), (1,)), ((), ())),
                                    preferred_element_type=F32)
        ylo_scr[pl.ds(r, nch, stride=S5_CHUNK), :] = y_r[:, :LANES]
        yhi_scr[pl.ds(r, nch, stride=S5_CHUNK), :] = y_r[:, LANES:]
    y = jnp.concatenate([ylo_scr[...], yhi_scr[...]], axis=1) + d_ref[...] * u_ref[...]
    y = _gelu_tanh(y)
    y = y * _sigmoid(_dot(y.astype(BF16), wglu_ref[...]) + bglu_ref[...])
    o_ref[...] = (_rms_rows(y) * gain_ref[...]).astype(BF16)


def _s5_out(u, uc, sp, krev, vt, d_skip, w_glu, b_glu, gain, layer):
    t = u.shape[0]
    nch = S5_OUT_ROWS // S5_CHUNK
    row = pl.BlockSpec((S5_OUT_ROWS, S5_WIDTH), lambda i: (i, 0))
    chunk_rows = pl.BlockSpec((nch, S5_CAT), lambda i: (i, 0))
    return pl.pallas_call(
        _s5_out_kernel,
        grid=(t // S5_OUT_ROWS,),
        in_specs=[row, chunk_rows, chunk_rows,
                  _resident((S5_CAT, S5_WIDTH), layer),
                  _resident((S5_CAT, 2 * S5_CH), layer),
                  _resident((1, S5_WIDTH), layer),
                  _resident((S5_WIDTH, S5_WIDTH), layer),
                  _resident((1, S5_WIDTH), layer),
                  _resident((1, S5_WIDTH), layer)],
        out_specs=row,
        out_shape=jax.ShapeDtypeStruct((t, S5_WIDTH), BF16),
        scratch_shapes=[pltpu.VMEM((S5_OUT_ROWS, LANES), F32),
                        pltpu.VMEM((S5_OUT_ROWS, LANES), F32)],
        compiler_params=_params(48, 1),
        name="s5_out",
    )(u, uc, sp, krev, vt, d_skip, w_glu, b_glu, gain)


def _lru_kernel(x_ref, gate_ref, cw_ref, cb_ref, wax_ref, bax_ref, lam_ref, gain_ref, *refs,
                cast_layer):
    if cast_layer is None:
        o_ref, ext_scr, a_scr, b_scr, h_scr, carry_scr = refs
    else:
        srcs, o_ref, dsts = refs[:3], refs[3], refs[4:7]
        ext_scr, a_scr, b_scr, h_scr, carry_scr = refs[7:12]
        cast_scratch = refs[12:]
        step = pl.program_id(0) * pl.num_programs(1) + pl.program_id(1)
        _cast_begin(step, pl.num_programs(0) * pl.num_programs(1), cast_layer, srcs, dsts,
                    cast_scratch)
    j = pl.program_id(1)
    rows = x_ref.shape[0]
    pad = SUBLANES

    @pl.when(j == 0)
    def _():
        ext_scr[0:pad, :] = jnp.zeros((pad, LRU_WIDTH), F32)
        carry_scr[...] = jnp.zeros(carry_scr.shape, F32)

    @pl.when(j > 0)
    def _():
        ext_scr[0:pad, :] = ext_scr[rows:rows + pad, :]

    ext_scr[pad:rows + pad, :] = x_ref[...]
    xc = cb_ref[...] + cw_ref[CONV_WIDTH - 1:CONV_WIDTH, :] * ext_scr[pad:rows + pad, :]
    for lag in range(1, CONV_WIDTH):
        tap = cw_ref[CONV_WIDTH - 1 - lag:CONV_WIDTH - lag, :]
        xc = xc + tap * ext_scr[pad - lag:rows + pad - lag, :]

    gates = _sigmoid(_dot(xc.astype(BF16), wax_ref[...]) + bax_ref[...])
    r = gates[:, :LRU_WIDTH]
    i = gates[:, LRU_WIDTH:]
    z = -lam_ref[...]
    softplus = jnp.maximum(z, 0.0) + jnp.log1p(jnp.exp(-jnp.abs(z)))
    a = jnp.exp((-LRU_C) * r * softplus)
    a_scr[...] = a
    b_scr[...] = jnp.sqrt(1.0 - a * a) * (i * xc)

    row = lax.broadcasted_iota(jnp.int32, (SUBLANES, LRU_WIDTH), 0)

    def tile_body(t, prev):
        off = pl.multiple_of(t * SUBLANES, SUBLANES)
        at = a_scr[pl.ds(off, SUBLANES), :]
        bt = b_scr[pl.ds(off, SUBLANES), :]
        for d in (1, 2, 4):
            keep = row >= d
            ar = jnp.where(keep, pltpu.roll(at, d, 0), 1.0)
            br = jnp.where(keep, pltpu.roll(bt, d, 0), 0.0)
            bt = at * br + bt
            at = at * ar
        h = at * prev + bt
        h_scr[pl.ds(off, SUBLANES), :] = h
        return jnp.broadcast_to(h[SUBLANES - 1:SUBLANES, :], (SUBLANES, LRU_WIDTH))

    carry_scr[...] = lax.fori_loop(0, rows // SUBLANES, tile_body, carry_scr[...],
                                   unroll=LRU_UNROLL)

    y = h_scr[...] * _gelu_tanh(gate_ref[...])
    o_ref[...] = (_rms_rows(y) * gain_ref[...]).astype(BF16)
    if cast_layer is not None:
        _cast_end(step, cast_layer, srcs, dsts, cast_scratch)


def _lru(x, gate, conv_w, conv_b, wax, bax, lam, gain, layer, batch, next_ffn_f32=None):
    t = x.shape[0]
    nblk = t // batch // LRU_ROWS
    row = pl.BlockSpec((LRU_ROWS, LRU_WIDTH), lambda b, j: (b * nblk + j, 0))
    in_specs = [row, row,
                _resident((CONV_WIDTH, LRU_WIDTH), layer),
                _resident((1, LRU_WIDTH), layer),
                _resident((LRU_WIDTH, 2 * LRU_WIDTH), layer),
                _resident((1, 2 * LRU_WIDTH), layer),
                _resident((1, LRU_WIDTH), layer),
                _resident((1, LRU_WIDTH), layer)]
    out_specs = [row]
    out_shape = [jax.ShapeDtypeStruct((t, LRU_WIDTH), BF16)]
    scratch = [pltpu.VMEM((LRU_ROWS + SUBLANES, LRU_WIDTH), F32),
               pltpu.VMEM((LRU_ROWS, LRU_WIDTH), F32),
               pltpu.VMEM((LRU_ROWS, LRU_WIDTH), F32),
               pltpu.VMEM((LRU_ROWS, LRU_WIDTH), F32),
               pltpu.VMEM((SUBLANES, LRU_WIDTH), F32)]
    operands = [x, gate, conv_w, conv_b, wax, bax, lam, gain]
    if next_ffn_f32 is not None:
        cast_in, cast_out, cast_shape, cast_scratch = _cast_plan(batch * nblk)
        in_specs += cast_in
        out_specs += cast_out
        out_shape += cast_shape
        scratch += cast_scratch
        operands += list(next_ffn_f32)
    outs = pl.pallas_call(
        functools.partial(_lru_kernel,
                          cast_layer=None if next_ffn_f32 is None else layer + 1),
        grid=(batch, nblk),
        in_specs=in_specs,
        out_specs=out_specs,
        out_shape=out_shape,
        scratch_shapes=scratch,
        compiler_params=_params(52, 2),
        name="lru",
    )(*operands)
    return outs[0], tuple(outs[1:])


def _ret_kernel(q_ref, kt_ref, v_ref, g_ref, cos_ref, sin_ref, cost_ref, sint_ref, decay_ref,
                zetat_ref, xi_ref, gamma_ref, gain_ref, o_ref, state_scr, y_scr):
    j = pl.program_id(1)

    @pl.when(j == 0)
    def _():
        state_scr[...] = jnp.zeros(state_scr.shape, F32)

    half = RET_HEAD_DIM // 2

    def chunk(start):
        rows = pl.ds(start, RET_CHUNK)
        cs = cos_ref[rows, :]
        sn = sin_ref[rows, :]
        cst = cost_ref[:, rows]
        snt = sint_ref[:, rows]
        ssq = jnp.zeros((RET_CHUNK, 1), F32)
        for h in range(RET_HEADS):
            cols = slice(h * RET_HEAD_DIM, (h + 1) * RET_HEAD_DIM)
            qh = q_ref[rows, cols]
            kt = kt_ref[cols, rows]
            vh = v_ref[rows, cols].astype(BF16)
            qb = (qh * cs + pltpu.roll(qh, half, 1) * sn).astype(BF16)
            kr = kt * cst + jnp.concatenate([kt[half:], kt[:half]], axis=0) * snt
            scores = _dot(qb, kr.astype(BF16)) * decay_ref[:, cols]
            inner = _dot(scores.astype(BF16), vh)
            st = state_scr[:, cols]
            cross = _dot(qb, st.astype(BF16)) * xi_ref[:, cols]
            kz = (kr * zetat_ref[cols, :]).astype(BF16)
            state_scr[:, cols] = gamma_ref[:, cols] * st + _dot(kz, vh)
            yh = _silu(g_ref[rows, cols]) * _rms_rows(inner + cross)
            ssq = ssq + jnp.sum(yh * yh, axis=-1, keepdims=True)
            y_scr[:, cols] = yh
        inv = lax.rsqrt(ssq * (1.0 / RET_WIDTH) + NORM_EPS)
        o_ref[rows, :] = ((y_scr[...] * inv) * gain_ref[...]).astype(BF16)

    def body(i, carry):
        for c in range(RET_UNROLL):
            chunk(pl.multiple_of((i * RET_UNROLL + c) * RET_CHUNK, RET_CHUNK))
        return carry

    lax.fori_loop(0, q_ref.shape[0] // (RET_CHUNK * RET_UNROLL), body, 0)


def _ret_tables(seq):
    dh, c, nh = RET_HEAD_DIM, RET_CHUNK, RET_HEADS
    pos = np.arange(seq, dtype=np.float64)
    inv_freq = ROPE_BASE ** (-np.arange(0, dh, 2, dtype=np.float64) / dh)
    ang = pos[:, None] * inv_freq[None, :]
    cos = np.cos(ang)
    sin = np.sin(ang)
    cos2 = np.concatenate([cos, cos], axis=-1)
    sin2 = np.concatenate([-sin, sin], axis=-1)
    log_gamma = np.log1p(-np.exp2(-5.0 - np.arange(nh, dtype=np.float64)))
    idx = np.arange(c, dtype=np.float64)
    diff = idx[:, None] - idx[None, :]
    decay = np.where(diff[None] >= 0,
                     np.exp(np.maximum(diff, 0.0)[None] * log_gamma[:, None, None]), 0.0)
    zeta = np.exp((c - 1.0 - idx)[None] * log_gamma[:, None])
    xi = np.exp((idx + 1.0)[None] * log_gamma[:, None])
    gamma_chunk = np.exp(c * log_gamma)
    decay_l = decay.transpose(1, 0, 2).reshape(c, nh * c)
    zeta_t = np.broadcast_to(zeta[:, None, :], (nh, dh, c)).reshape(nh * dh, c)
    xi_l = np.broadcast_to(xi.T[:, :, None], (c, nh, dh)).reshape(c, nh * dh)
    gamma_l = np.broadcast_to(gamma_chunk[:, None], (nh, dh)).reshape(1, nh * dh)
    scale = dh ** -0.5
    tables = (cos2 * scale, sin2 * scale, cos2.T, sin2.T, decay_l, zeta_t, xi_l, gamma_l)
    return tuple(jnp.asarray(np.ascontiguousarray(a, dtype=np.float32)) for a in tables)


def _ret(q, k_t, v, g, tables, gain, layer, batch):
    t = q.shape[0]
    nblk = t // batch // RET_ROWS
    row = pl.BlockSpec((RET_ROWS, RET_WIDTH), lambda b, j: (b * nblk + j, 0))
    row_t = pl.BlockSpec((RET_WIDTH, RET_ROWS), lambda b, j: (0, b * nblk + j))
    rot = pl.BlockSpec((RET_ROWS, RET_HEAD_DIM), lambda b, j: (j, 0))
    rot_t = pl.BlockSpec((RET_HEAD_DIM, RET_ROWS), lambda b, j: (0, j))
    const = lambda shape: pl.BlockSpec(shape, lambda b, j: (0, 0), pipeline_mode=pl.Buffered(1))
    return pl.pallas_call(
        _ret_kernel,
        grid=(batch, nblk),
        in_specs=[row, row_t, row, row, rot, rot, rot_t, rot_t,
                  const((RET_CHUNK, RET_WIDTH)), const((RET_WIDTH, RET_CHUNK)),
                  const((RET_CHUNK, RET_WIDTH)), const((1, RET_WIDTH)),
                  _resident((1, RET_WIDTH), layer)],
        out_specs=row,
        out_shape=jax.ShapeDtypeStruct((t, RET_WIDTH), BF16),
        scratch_shapes=[pltpu.VMEM((RET_HEAD_DIM, RET_WIDTH), F32),
                        pltpu.VMEM((RET_CHUNK, RET_WIDTH), F32)],
        compiler_params=_params(32, 2),
        name="ret",
    )(q, k_t, v, g, *tables, gain)


def _block_diag(w):
    depth, n, d, e = w.shape
    eye = jnp.eye(n, dtype=w.dtype)
    return jnp.einsum('lnde,nm->lndme', w, eye).reshape(depth, n * d, n * e)


def kernel(x, ffn1_norm, ffn1_w_gate, ffn1_w_up, ffn1_w_down, mix_norm, w_in, s5_lambda_re, s5_lambda_im, s5_log_step, s5_b_re, s5_b_im, s5_c_re, s5_c_im, s5_d, s5_w_glu, s5_b_glu, s5_out_norm, ret_out_norm, lru_conv_w, lru_conv_b, lru_w_a, lru_b_a, lru_w_x, lru_b_x, lru_lambda, lru_out_norm, w_out, ffn2_norm, ffn2_w_gate, ffn2_w_up, ffn2_w_down, final_norm):
    batch, seq, d = x.shape
    depth = w_in.shape[0]
    t = batch * seq
    assert d == D_MODEL and seq % max(S5_STATE_ROWS, LRU_ROWS, RET_ROWS) == 0
    assert t % max(FFN_ROWS, PROJ_ROWS, S5_OUT_ROWS) == 0

    row3 = lambda a: a.reshape(depth, 1, a.shape[-1])
    bf = lambda a: a.astype(BF16)

    ffn1_f32 = (ffn1_w_gate, ffn1_w_up, ffn1_w_down)
    ffn2_f32 = (ffn2_w_gate, ffn2_w_up, ffn2_w_down)
    ffn1_b = tuple(bf(w[0]) for w in ffn1_f32)
    w_in_b = bf(w_in)
    k_off = sum(IN_SECTIONS[:K_SECTION])
    w_k = lax.optimization_barrier(w_in[:, :, k_off:k_off + RET_WIDTH])
    w_k_t = bf(jnp.swapaxes(w_k, 1, 2))
    w_out_b = bf(w_out)
    s5_wst, s5_krev, s5_vt, s5_consts = _s5_prepare(s5_lambda_re, s5_lambda_im, s5_log_step,
                                                 s5_b_re, s5_b_im, s5_c_re, s5_c_im)
    s5_glu = bf(s5_w_glu)
    lru_wax = bf(jnp.concatenate([_block_diag(lru_w_a), _block_diag(lru_w_x)], axis=-1))
    lru_bax = row3(jnp.concatenate([lru_b_a, lru_b_x], axis=-1))
    tables = _ret_tables(seq)

    xt = x.reshape(t, d)
    for l in range(depth):
        xt = _ffn(xt, row3(ffn1_norm), ffn1_b, l)
        (u, q, k_t, v, g, xl, gl), ffn2_b = _inproj(xt, row3(mix_norm), w_in_b, w_k_t, ffn2_f32, l)
        uc, sp = _s5_state(u, s5_wst, s5_consts, l, batch)
        y_s5 = _s5_out(u, uc, sp, s5_krev, s5_vt, row3(s5_d), s5_glu, row3(s5_b_glu),
                       row3(s5_out_norm), l)
        y_ret = _ret(q, k_t, v, g, tables, row3(ret_out_norm), l, batch)
        y_lru, ffn1_b = _lru(xl, gl, lru_conv_w, row3(lru_conv_b), lru_wax, lru_bax,
                             row3(lru_lambda), row3(lru_out_norm), l, batch,
                             next_ffn_f32=ffn1_f32 if l + 1 < depth else None)
        last = final_norm.reshape(1, d) if l == depth - 1 else None
        xt = _ffn(xt, row3(ffn2_norm), ffn2_b, l, mix=(y_s5, y_ret, y_lru, w_out_b),
                  final_gain=last)
    return xt.reshape(batch, seq, d)
```

```python
import functools
import math

import jax
import jax.numpy as jnp
import numpy as np
from jax import lax
from jax.experimental import pallas as pl
from jax.experimental.pallas import tpu as pltpu

F32 = jnp.float32
BF16 = jnp.bfloat16

D_MODEL = 1024
D_FF = 2816
S5_WIDTH = 256
S5_GROUP_DIM = 16
S5_GROUPS = 16
S5_STATE = 64
S5_CH = S5_GROUPS * S5_STATE
RET_WIDTH = 512
RET_HEAD_DIM = 128
RET_HEADS = 4
RET_CHUNK = 128
LRU_WIDTH = 256
LRU_BLOCKS = 4
LRU_BLOCK_DIM = 64
CONV_WIDTH = 4
LRU_C = 8.0
ROPE_BASE = 10000.0
NORM_EPS = 1e-6
IN_SECTIONS = (S5_WIDTH, RET_WIDTH, RET_WIDTH, RET_WIDTH, RET_WIDTH, LRU_WIDTH, LRU_WIDTH)
IN_WIDTH = sum(IN_SECTIONS)

SUBLANES = 8
LANES = 128
MIB = 1024 * 1024

FFN_ROWS = 1024
FFN_CHUNK = 256
PROJ_ROWS = 1024
S5_STATE_ROWS = 4096
S5_OUT_ROWS = 2048
LRU_ROWS = 2048
LRU_UNROLL = 4
RET_ROWS = 1024
RET_UNROLL = 2


def _params(vmem_mib, n_axes):
    return pltpu.CompilerParams(
        dimension_semantics=("arbitrary",) * n_axes,
        vmem_limit_bytes=vmem_mib * MIB)


def _rms_rows(x):
    return x * lax.rsqrt(jnp.mean(x * x, axis=-1, keepdims=True) + NORM_EPS)


def _gelu_tanh(x):
    c = math.sqrt(2.0 / math.pi)
    return x * (0.5 * (1.0 + jnp.tanh(c * (x + 0.044715 * (x * x * x)))))


def _sigmoid(x):
    return 0.5 + 0.5 * jnp.tanh(0.5 * x)


def _silu(x):
    h = 0.5 * x
    return h + h * jnp.tanh(h)


def _dot(a, b):
    return jnp.dot(a, b, preferred_element_type=F32)


def _resident(shape, layer):
    nd = len(shape)
    return pl.BlockSpec((None,) + tuple(shape), lambda *_: (layer,) + (0,) * nd,
                        pipeline_mode=pl.Buffered(1))


def _ffn_kernel(*refs, with_outproj, with_final_norm):
    refs = list(refs)
    o_ref = refs.pop()
    x = refs.pop(0)[...]
    if with_outproj:
        ys_ref, yr_ref, yl_ref, wo_ref = refs[:4]
        refs = refs[4:]
        x = x + _dot(ys_ref[...], wo_ref[0:S5_WIDTH, :])
        x = x + _dot(yr_ref[...], wo_ref[S5_WIDTH:S5_WIDTH + RET_WIDTH, :])
        x = x + _dot(yl_ref[...], wo_ref[S5_WIDTH + RET_WIDTH:, :])
    g_ref, wg_ref, wu_ref, wd_ref = refs[:4]
    h = (_rms_rows(x) * g_ref[...]).astype(BF16)
    acc = jnp.zeros(x.shape, F32)
    for c in range(D_FF // FFN_CHUNK):
        sl = slice(c * FFN_CHUNK, (c + 1) * FFN_CHUNK)
        act = (_silu(_dot(h, wg_ref[:, sl])) * _dot(h, wu_ref[:, sl])).astype(BF16)
        acc = acc + _dot(act, wd_ref[sl, :])
    y = x + 0.5 * acc
    if with_final_norm:
        y = _rms_rows(y) * refs[4][...]
    o_ref[...] = y


def _ffn(x, gain, weights, layer, mix=None, final_gain=None):
    t = x.shape[0]
    spec = lambda w: pl.BlockSpec((FFN_ROWS, w), lambda i: (i, 0))
    whole = lambda a: pl.BlockSpec(a.shape, lambda i: (0, 0), pipeline_mode=pl.Buffered(1))
    operands = [x]
    in_specs = [spec(D_MODEL)]
    if mix is not None:
        y_s5, y_ret, y_lru, w_out = mix
        operands += [y_s5, y_ret, y_lru, w_out]
        in_specs += [spec(S5_WIDTH), spec(RET_WIDTH), spec(LRU_WIDTH),
                     _resident((D_MODEL, D_MODEL), layer)]
    operands += [gain, *weights]
    in_specs += [_resident((1, D_MODEL), layer)] + [whole(w) for w in weights]
    if final_gain is not None:
        operands.append(final_gain)
        in_specs.append(pl.BlockSpec((1, D_MODEL), lambda i: (0, 0)))
    return pl.pallas_call(
        functools.partial(_ffn_kernel, with_outproj=mix is not None,
                          with_final_norm=final_gain is not None),
        grid=(t // FFN_ROWS,),
        in_specs=in_specs,
        out_specs=spec(D_MODEL),
        out_shape=jax.ShapeDtypeStruct(x.shape, F32),
        compiler_params=_params(56 if mix is not None else 50, 1),
        name="ffn_mix" if mix is not None else "ffn",
    )(*operands)


FFN_WEIGHT_SHAPES = ((D_MODEL, D_FF), (D_MODEL, D_FF), (D_FF, D_MODEL))


def _cast_plan(nsteps):
    slabs = [(r // nsteps, c) for r, c in FFN_WEIGHT_SHAPES]
    hbm = pl.BlockSpec(memory_space=pl.ANY)
    scratch = ([pltpu.VMEM((2,) + sl, F32) for sl in slabs] + [pltpu.VMEM(sl, BF16) for sl in slabs]
               + [pltpu.SemaphoreType.DMA((3, 3))])
    out_shape = [jax.ShapeDtypeStruct(shape, BF16) for shape in FFN_WEIGHT_SHAPES]
    return [hbm] * 3, [hbm] * 3, out_shape, scratch


def _cast_copies(layer, srcs, dsts, scratch):
    stage_in, stage_out, sem = scratch[:3], scratch[3:6], scratch[6]

    def rows(i, s):
        n = stage_out[i].shape[0]
        return pl.ds(pl.multiple_of(s * n, 16), n)

    fetch = lambda i, s: pltpu.make_async_copy(srcs[i].at[layer, rows(i, s), :],
                                               stage_in[i].at[s % 2], sem.at[i, s % 2])
    put = lambda i, s: pltpu.make_async_copy(stage_out[i], dsts[i].at[rows(i, s), :], sem.at[i, 2])
    return fetch, put, stage_in, stage_out


def _cast_begin(step, nsteps, layer, srcs, dsts, scratch):
    fetch, put, stage_in, stage_out = _cast_copies(layer, srcs, dsts, scratch)

    @pl.when(step == 0)
    def _():
        for i in range(3):
            fetch(i, 0).start()

    for i in range(3):
        fetch(i, step).wait()
        stage_out[i][...] = stage_in[i][step % 2].astype(BF16)
        put(i, step).start()

    @pl.when(step + 1 < nsteps)
    def _():
        for i in range(3):
            fetch(i, step + 1).start()


def _cast_end(step, layer, srcs, dsts, scratch):
    _, put, _, _ = _cast_copies(layer, srcs, dsts, scratch)
    for i in range(3):
        put(i, step).wait()


K_SECTION = 2


def _inproj_kernel(x_ref, g_ref, w_ref, wkt_ref, *out_refs):
    h = (_rms_rows(x_ref[...]) * g_ref[...]).astype(BF16)
    off = 0
    for n, (o_ref, width) in enumerate(zip(out_refs, IN_SECTIONS)):
        if n == K_SECTION:
            o_ref[...] = lax.dot_general(wkt_ref[...], h, (((1,), (1,)), ((), ())),
                                         preferred_element_type=F32)
        else:
            o_ref[...] = _dot(h, w_ref[:, off:off + width])
        off += width


def _inproj(x, gain, w_in, w_k_t, layer):
    t = x.shape[0]
    row = lambda w: pl.BlockSpec((PROJ_ROWS, w), lambda i: (i, 0))
    out_specs = [row(w) for w in IN_SECTIONS]
    out_shape = [jax.ShapeDtypeStruct((t, w), F32) for w in IN_SECTIONS]
    out_specs[K_SECTION] = pl.BlockSpec((RET_WIDTH, PROJ_ROWS), lambda i: (0, i))
    out_shape[K_SECTION] = jax.ShapeDtypeStruct((RET_WIDTH, t), F32)
    return pl.pallas_call(
        _inproj_kernel,
        grid=(t // PROJ_ROWS,),
        in_specs=[row(D_MODEL),
                  _resident((1, D_MODEL), layer),
                  _resident((D_MODEL, IN_WIDTH), layer),
                  _resident((RET_WIDTH, D_MODEL), layer)],
        out_specs=out_specs,
        out_shape=out_shape,
        compiler_params=_params(52, 1),
        name="inproj",
    )(x, gain, w_in, w_k_t)


S5_CHUNK = 8
S5_CAT = S5_CHUNK * S5_WIDTH
S5_CONST_ROWS = 8 * SUBLANES


def _cmul(a, b):
    return a[0] * b[0] - a[1] * b[1], a[0] * b[1] + a[1] * b[0]


def _s5_prep_kernel(lr_ref, li_ref, ls_ref, er_ref, ei_ref, ctr_ref, cti_ref, cst_ref,
                    wst_ref, kk_ref, vt_ref, c_ref):
    k = pl.program_id(1)
    lr = lr_ref[...]
    li = li_ref[...]
    step = jnp.exp(ls_ref[...])
    ar = lr * step
    ai = li * step
    mag = jnp.exp(ar)
    p1 = (mag * jnp.cos(ai), mag * jnp.sin(ai))
    nr = p1[0] - 1.0
    den = lr * lr + li * li
    f = ((nr * lr + p1[1] * li) / den, (p1[1] * lr - nr * li) / den)
    powers = [(jnp.ones_like(lr), jnp.zeros_like(lr)), p1]
    for _ in range(2, S5_CHUNK + 1):
        powers.append(_cmul(powers[-1], p1))

    def pick(n):
        out = powers[0]
        for i in range(1, S5_CHUNK + 1):
            out = (jnp.where(n == i, powers[i][0], out[0]), jnp.where(n == i, powers[i][1], out[1]))
        return out

    fp = _cmul(f, pick(k))
    er = er_ref[...]
    ei = ei_ref[...]
    w = jnp.concatenate([fp[0] * er - fp[1] * ei, fp[0] * ei + fp[1] * er], axis=1)
    wb = w.astype(BF16)
    wst_ref[...] = wb
    kk_ref[...] = _dot(wb, cst_ref[...]).astype(BF16)
    a, b = pick(k + 1)
    ctr = ctr_ref[...]
    cti = cti_ref[...]
    vt_ref[:, :S5_CH] = (ctr * a - cti * b).astype(BF16)
    vt_ref[:, S5_CH:] = (-(ctr * b) - cti * a).astype(BF16)

    @pl.when(k == 0)
    def _():
        q = [powers[S5_CHUNK]]
        for _ in range(1, SUBLANES):
            q.append(_cmul(q[-1], q[0]))
        row = lax.broadcasted_iota(jnp.int32, (SUBLANES, S5_CH), 0)
        zero = jnp.zeros((SUBLANES, S5_CH), F32)
        for s, d in enumerate((1, 2, 4)):
            for part in range(2):
                base = (2 * s + part) * SUBLANES
                c_ref[base:base + SUBLANES, :] = jnp.where(
                    row >= d, jnp.broadcast_to(q[d - 1][part], (SUBLANES, S5_CH)), zero)
        for part in range(2):
            tile = zero
            for r in range(SUBLANES):
                tile = jnp.where(row == r, jnp.broadcast_to(q[r][part], (SUBLANES, S5_CH)), tile)
            base = (6 + part) * SUBLANES
            c_ref[base:base + SUBLANES, :] = tile


def _s5_prepare(lam_re, lam_im, log_step, b_re, b_im, c_re, c_im):
    depth = lam_re.shape[0]
    def embed(a):
        n, m = a.shape[2:]
        same = np.equal.outer(np.repeat(np.arange(S5_GROUPS), n), np.arange(S5_GROUPS))
        tiled = jnp.broadcast_to(a.reshape(depth, S5_GROUPS * n, 1, m),
                                 (depth, S5_GROUPS * n, S5_GROUPS, m))
        return jnp.where(same[None, :, :, None], tiled, 0.0).reshape(
            depth, S5_GROUPS * n, S5_GROUPS * m)

    b_rows = lambda b: embed(jnp.swapaxes(b, 2, 3))
    c_rows = embed
    c_cols = lambda c: embed(jnp.swapaxes(c, 2, 3))
    c_stack = jnp.concatenate([c_cols(c_re), -c_cols(c_im)], axis=1).astype(BF16)
    vec = lambda a: a.reshape(depth, 1, S5_CH)
    ls = jnp.broadcast_to(log_step[:, :, None], (depth, S5_GROUPS, S5_STATE))
    vspec = pl.BlockSpec((None, 1, S5_CH), lambda l, k: (l, 0, 0))
    mspec = pl.BlockSpec((None, S5_WIDTH, S5_CH), lambda l, k: (l, 0, 0))
    reverse = lambda l, k: (l, S5_CHUNK - 1 - k, 0)
    return pl.pallas_call(
        _s5_prep_kernel,
        grid=(depth, S5_CHUNK),
        in_specs=[vspec, vspec, vspec, mspec, mspec, mspec, mspec,
                  pl.BlockSpec((None, 2 * S5_CH, S5_WIDTH), lambda l, k: (l, 0, 0))],
        out_specs=[pl.BlockSpec((None, S5_WIDTH, 2 * S5_CH), reverse),
                   pl.BlockSpec((None, S5_WIDTH, S5_WIDTH), reverse),
                   pl.BlockSpec((None, S5_WIDTH, 2 * S5_CH), lambda l, k: (l, k, 0)),
                   pl.BlockSpec((None, S5_CONST_ROWS, S5_CH), lambda l, k: (l, 0, 0))],
        out_shape=[jax.ShapeDtypeStruct((depth, S5_CAT, 2 * S5_CH), BF16),
                   jax.ShapeDtypeStruct((depth, S5_CAT, S5_WIDTH), BF16),
                   jax.ShapeDtypeStruct((depth, S5_CAT, 2 * S5_CH), BF16),
                   jax.ShapeDtypeStruct((depth, S5_CONST_ROWS, S5_CH), F32)],
        compiler_params=_params(40, 2),
        name="s5_prep",
    )(vec(lam_re), vec(lam_im), vec(ls), b_rows(b_re), b_rows(b_im), c_rows(c_re), c_rows(c_im),
      c_stack)


def _s5_state_kernel(ulo_ref, uhi_ref, wst_ref, k_ref, uc_ref, sp_ref, ds_scr, s_scr):
    j = pl.program_id(1)
    nch = ulo_ref.shape[0] // S5_CHUNK
    ncol = S5_CH // LANES
    cols = [(slice(cb * LANES, (cb + 1) * LANES),
             slice(S5_CH + cb * LANES, S5_CH + (cb + 1) * LANES)) for cb in range(ncol)]

    @pl.when(j == 0)
    def _():
        s_scr[0:SUBLANES, :] = jnp.zeros((SUBLANES, 2 * S5_CH), F32)

    @pl.when(j > 0)
    def _():
        s_scr[0:SUBLANES, :] = s_scr[nch:nch + SUBLANES, :]

    uc = jnp.concatenate([half[pl.ds(jj, nch, stride=S5_CHUNK), :]
                          for jj in range(S5_CHUNK) for half in (ulo_ref, uhi_ref)],
                         axis=1).astype(BF16)
    uc_ref[...] = uc
    ds_scr[...] = _dot(uc, wst_ref[...])

    def tile_body(i, carry):
        off = pl.multiple_of(i * SUBLANES, SUBLANES)
        loaded = [(ds_scr[pl.ds(off, SUBLANES), re], ds_scr[pl.ds(off, SUBLANES), im])
                  for re, im in cols]
        done = []
        for cb, (br, bi) in enumerate(loaded):
            re = cols[cb][0]
            for s, d in enumerate((1, 2, 4)):
                cr = k_ref[(2 * s) * SUBLANES:(2 * s + 1) * SUBLANES, re]
                ci = k_ref[(2 * s + 1) * SUBLANES:(2 * s + 2) * SUBLANES, re]
                rr = pltpu.roll(br, d, 0)
                ri = pltpu.roll(bi, d, 0)
                br, bi = br + (cr * rr - ci * ri), bi + (cr * ri + ci * rr)
            pr, pi = carry[cb], carry[ncol + cb]
            wr = k_ref[6 * SUBLANES:7 * SUBLANES, re]
            wi = k_ref[7 * SUBLANES:8 * SUBLANES, re]
            done.append((br + (wr * pr - wi * pi), bi + (wr * pi + wi * pr)))
        for (re, im), (sr, si) in zip(cols, done):
            s_scr[pl.ds(off + SUBLANES, SUBLANES), re] = sr
            s_scr[pl.ds(off + SUBLANES, SUBLANES), im] = si
        return tuple(last(sr) for sr, _ in done) + tuple(last(si) for _, si in done)

    last = lambda a: jnp.broadcast_to(a[SUBLANES - 1:SUBLANES, :], (SUBLANES, LANES))
    init = (tuple(last(s_scr[0:SUBLANES, re]) for re, _ in cols)
            + tuple(last(s_scr[0:SUBLANES, im]) for _, im in cols))
    lax.fori_loop(0, nch // SUBLANES, tile_body, init)

    sp_ref[...] = s_scr[SUBLANES - 1:SUBLANES - 1 + nch, :].astype(BF16)


def _s5_state(u, wst, consts, layer, batch):
    t = u.shape[0]
    nblk = t // batch // S5_STATE_ROWS
    nch = S5_STATE_ROWS // S5_CHUNK
    chunk_rows = pl.BlockSpec((nch, S5_CAT), lambda b, j: (b * nblk + j, 0))
    return pl.pallas_call(
        _s5_state_kernel,
        grid=(batch, nblk),
        in_specs=[pl.BlockSpec((S5_STATE_ROWS, LANES), lambda b, j: (b * nblk + j, 0)),
                  pl.BlockSpec((S5_STATE_ROWS, LANES), lambda b, j: (b * nblk + j, 1)),
                  _resident((S5_CAT, 2 * S5_CH), layer),
                  _resident((S5_CONST_ROWS, S5_CH), layer)],
        out_specs=[chunk_rows, chunk_rows],
        out_shape=[jax.ShapeDtypeStruct((t // S5_CHUNK, S5_CAT), BF16),
                   jax.ShapeDtypeStruct((t // S5_CHUNK, 2 * S5_CH), BF16)],
        scratch_shapes=[pltpu.VMEM((nch, 2 * S5_CH), F32),
                        pltpu.VMEM((nch + SUBLANES, 2 * S5_CH), F32)],
        compiler_params=_params(44, 2),
        name="s5_state",
    )(u, u, wst, consts)


def _s5_out_kernel(u_ref, uc_ref, sp_ref, krev_ref, vt_ref, d_ref, wglu_ref, bglu_ref, gain_ref,
                   *refs, cast_layer):
    srcs, o_ref, dsts, ylo_scr, yhi_scr, cast_scratch = (refs[:3], refs[3], refs[4:7], refs[7],
                                                         refs[8], refs[9:])
    step = pl.program_id(0)
    _cast_begin(step, pl.num_programs(0), cast_layer, srcs, dsts, cast_scratch)
    nch = uc_ref.shape[0]
    sp = sp_ref[...]
    for r in range(S5_CHUNK):
        cols = slice(r * S5_WIDTH, (r + 1) * S5_WIDTH)
        used = (r + 1) * S5_WIDTH
        y_r = _dot(uc_ref[:, :used], krev_ref[S5_CAT - used:, :])
        y_r = y_r + lax.dot_general(sp, vt_ref[cols, :], (((1,), (1,)), ((), ())),
                                    preferred_element_type=F32)
        ylo_scr[pl.ds(r, nch, stride=S5_CHUNK), :] = y_r[:, :LANES]
        yhi_scr[pl.ds(r, nch, stride=S5_CHUNK), :] = y_r[:, LANES:]
    y = jnp.concatenate([ylo_scr[...], yhi_scr[...]], axis=1) + d_ref[...] * u_ref[...]
    y = _gelu_tanh(y)
    y = y * _sigmoid(_dot(y.astype(BF16), wglu_ref[...]) + bglu_ref[...])
    o_ref[...] = (_rms_rows(y) * gain_ref[...]).astype(BF16)
    _cast_end(step, cast_layer, srcs, dsts, cast_scratch)


def _s5_out(u, uc, sp, krev, vt, d_skip, w_glu, b_glu, gain, ffn_f32, layer):
    t = u.shape[0]
    nsteps = t // S5_OUT_ROWS
    nch = S5_OUT_ROWS // S5_CHUNK
    row = pl.BlockSpec((S5_OUT_ROWS, S5_WIDTH), lambda i: (i, 0))
    chunk_rows = pl.BlockSpec((nch, S5_CAT), lambda i: (i, 0))
    cast_in, cast_out, cast_shape, cast_scratch = _cast_plan(nsteps)
    outs = pl.pallas_call(
        functools.partial(_s5_out_kernel, cast_layer=layer),
        grid=(nsteps,),
        in_specs=[row, chunk_rows, chunk_rows,
                  _resident((S5_CAT, S5_WIDTH), layer),
                  _resident((S5_CAT, 2 * S5_CH), layer),
                  _resident((1, S5_WIDTH), layer),
                  _resident((S5_WIDTH, S5_WIDTH), layer),
                  _resident((1, S5_WIDTH), layer),
                  _resident((1, S5_WIDTH), layer)] + cast_in,
        out_specs=[row] + cast_out,
        out_shape=[jax.ShapeDtypeStruct((t, S5_WIDTH), BF16)] + cast_shape,
        scratch_shapes=[pltpu.VMEM((S5_OUT_ROWS, LANES), F32),
                        pltpu.VMEM((S5_OUT_ROWS, LANES), F32)] + cast_scratch,
        compiler_params=_params(52, 1),
        name="s5_out",
    )(u, uc, sp, krev, vt, d_skip, w_glu, b_glu, gain, *ffn_f32)
    return outs[0], tuple(outs[1:])


def _lru_kernel(x_ref, gate_ref, cw_ref, cb_ref, wax_ref, bax_ref, lam_ref, gain_ref, *refs,
                cast_layer):
    if cast_layer is None:
        o_ref, ext_scr, a_scr, b_scr, h_scr, carry_scr = refs
    else:
        srcs, o_ref, dsts = refs[:3], refs[3], refs[4:7]
        ext_scr, a_scr, b_scr, h_scr, carry_scr = refs[7:12]
        cast_scratch = refs[12:]
        step = pl.program_id(0) * pl.num_programs(1) + pl.program_id(1)
        _cast_begin(step, pl.num_programs(0) * pl.num_programs(1), cast_layer, srcs, dsts,
                    cast_scratch)
    j = pl.program_id(1)
    rows = x_ref.shape[0]
    pad = SUBLANES

    @pl.when(j == 0)
    def _():
        ext_scr[0:pad, :] = jnp.zeros((pad, LRU_WIDTH), F32)
        carry_scr[...] = jnp.zeros(carry_scr.shape, F32)

    @pl.when(j > 0)
    def _():
        ext_scr[0:pad, :] = ext_scr[rows:rows + pad, :]

    ext_scr[pad:rows + pad, :] = x_ref[...]
    xc = cb_ref[...] + cw_ref[CONV_WIDTH - 1:CONV_WIDTH, :] * ext_scr[pad:rows + pad, :]
    for lag in range(1, CONV_WIDTH):
        tap = cw_ref[CONV_WIDTH - 1 - lag:CONV_WIDTH - lag, :]
        xc = xc + tap * ext_scr[pad - lag:rows + pad - lag, :]

    gates = _sigmoid(_dot(xc.astype(BF16), wax_ref[...]) + bax_ref[...])
    r = gates[:, :LRU_WIDTH]
    i = gates[:, LRU_WIDTH:]
    z = -lam_ref[...]
    softplus = jnp.maximum(z, 0.0) + jnp.log1p(jnp.exp(-jnp.abs(z)))
    a = jnp.exp((-LRU_C) * r * softplus)
    a_scr[...] = a
    b_scr[...] = jnp.sqrt(1.0 - a * a) * (i * xc)

    row = lax.broadcasted_iota(jnp.int32, (SUBLANES, LRU_WIDTH), 0)

    def tile_body(t, prev):
        off = pl.multiple_of(t * SUBLANES, SUBLANES)
        at = a_scr[pl.ds(off, SUBLANES), :]
        bt = b_scr[pl.ds(off, SUBLANES), :]
        for d in (1, 2, 4):
            keep = row >= d
            ar = jnp.where(keep, pltpu.roll(at, d, 0), 1.0)
            br = jnp.where(keep, pltpu.roll(bt, d, 0), 0.0)
            bt = at * br + bt
            at = at * ar
        h = at * prev + bt
        h_scr[pl.ds(off, SUBLANES), :] = h
        return jnp.broadcast_to(h[SUBLANES - 1:SUBLANES, :], (SUBLANES, LRU_WIDTH))

    carry_scr[...] = lax.fori_loop(0, rows // SUBLANES, tile_body, carry_scr[...],
                                   unroll=LRU_UNROLL)

    y = h_scr[...] * _gelu_tanh(gate_ref[...])
    o_ref[...] = (_rms_rows(y) * gain_ref[...]).astype(BF16)
    if cast_layer is not None:
        _cast_end(step, cast_layer, srcs, dsts, cast_scratch)


def _lru(x, gate, conv_w, conv_b, wax, bax, lam, gain, layer, batch, next_ffn_f32=None):
    t = x.shape[0]
    nblk = t // batch // LRU_ROWS
    row = pl.BlockSpec((LRU_ROWS, LRU_WIDTH), lambda b, j: (b * nblk + j, 0))
    in_specs = [row, row,
                _resident((CONV_WIDTH, LRU_WIDTH), layer),
                _resident((1, LRU_WIDTH), layer),
                _resident((LRU_WIDTH, 2 * LRU_WIDTH), layer),
                _resident((1, 2 * LRU_WIDTH), layer),
                _resident((1, LRU_WIDTH), layer),
                _resident((1, LRU_WIDTH), layer)]
    out_specs = [row]
    out_shape = [jax.ShapeDtypeStruct((t, LRU_WIDTH), BF16)]
    scratch = [pltpu.VMEM((LRU_ROWS + SUBLANES, LRU_WIDTH), F32),
               pltpu.VMEM((LRU_ROWS, LRU_WIDTH), F32),
               pltpu.VMEM((LRU_ROWS, LRU_WIDTH), F32),
               pltpu.VMEM((LRU_ROWS, LRU_WIDTH), F32),
               pltpu.VMEM((SUBLANES, LRU_WIDTH), F32)]
    operands = [x, gate, conv_w, conv_b, wax, bax, lam, gain]
    if next_ffn_f32 is not None:
        cast_in, cast_out, cast_shape, cast_scratch = _cast_plan(batch * nblk)
        in_specs += cast_in
        out_specs += cast_out
        out_shape += cast_shape
        scratch += cast_scratch
        operands += list(next_ffn_f32)
    outs = pl.pallas_call(
        functools.partial(_lru_kernel,
                          cast_layer=None if next_ffn_f32 is None else layer + 1),
        grid=(batch, nblk),
        in_specs=in_specs,
        out_specs=out_specs,
        out_shape=out_shape,
        scratch_shapes=scratch,
        compiler_params=_params(52, 2),
        name="lru",
    )(*operands)
    return outs[0], tuple(outs[1:])


def _ret_kernel(q_ref, kt_ref, v_ref, g_ref, cos_ref, sin_ref, cost_ref, sint_ref, decay_ref,
                zetat_ref, xi_ref, gamma_ref, gain_ref, o_ref, state_scr, y_scr):
    j = pl.program_id(1)

    @pl.when(j == 0)
    def _():
        state_scr[...] = jnp.zeros(state_scr.shape, F32)

    half = RET_HEAD_DIM // 2

    def chunk(start):
        rows = pl.ds(start, RET_CHUNK)
        cs = cos_ref[rows, :]
        sn = sin_ref[rows, :]
        cst = cost_ref[:, rows]
        snt = sint_ref[:, rows]
        ssq = jnp.zeros((RET_CHUNK, 1), F32)
        for h in range(RET_HEADS):
            cols = slice(h * RET_HEAD_DIM, (h + 1) * RET_HEAD_DIM)
            qh = q_ref[rows, cols]
            kt = kt_ref[cols, rows]
            vh = v_ref[rows, cols].astype(BF16)
            qb = (qh * cs + pltpu.roll(qh, half, 1) * sn).astype(BF16)
            kr = kt * cst + jnp.concatenate([kt[half:], kt[:half]], axis=0) * snt
            scores = _dot(qb, kr.astype(BF16)) * decay_ref[:, cols]
            inner = _dot(scores.astype(BF16), vh)
            st = state_scr[:, cols]
            cross = _dot(qb, st.astype(BF16)) * xi_ref[:, cols]
            kz = (kr * zetat_ref[cols, :]).astype(BF16)
            state_scr[:, cols] = gamma_ref[:, cols] * st + _dot(kz, vh)
            yh = _silu(g_ref[rows, cols]) * _rms_rows(inner + cross)
            ssq = ssq + jnp.sum(yh * yh, axis=-1, keepdims=True)
            y_scr[:, cols] = yh
        inv = lax.rsqrt(ssq * (1.0 / RET_WIDTH) + NORM_EPS)
        o_ref[rows, :] = ((y_scr[...] * inv) * gain_ref[...]).astype(BF16)

    def body(i, carry):
        for c in range(RET_UNROLL):
            chunk(pl.multiple_of((i * RET_UNROLL + c) * RET_CHUNK, RET_CHUNK))
        return carry

    lax.fori_loop(0, q_ref.shape[0] // (RET_CHUNK * RET_UNROLL), body, 0)


def _ret_tables(seq):
    dh, c, nh = RET_HEAD_DIM, RET_CHUNK, RET_HEADS
    pos = np.arange(seq, dtype=np.float64)
    inv_freq = ROPE_BASE ** (-np.arange(0, dh, 2, dtype=np.float64) / dh)
    ang = pos[:, None] * inv_freq[None, :]
    cos = np.cos(ang)
    sin = np.sin(ang)
    cos2 = np.concatenate([cos, cos], axis=-1)
    sin2 = np.concatenate([-sin, sin], axis=-1)
    log_gamma = np.log1p(-np.exp2(-5.0 - np.arange(nh, dtype=np.float64)))
    idx = np.arange(c, dtype=np.float64)
    diff = idx[:, None] - idx[None, :]
    decay = np.where(diff[None] >= 0,
                     np.exp(np.maximum(diff, 0.0)[None] * log_gamma[:, None, None]), 0.0)
    zeta = np.exp((c - 1.0 - idx)[None] * log_gamma[:, None])
    xi = np.exp((idx + 1.0)[None] * log_gamma[:, None])
    gamma_chunk = np.exp(c * log_gamma)
    decay_l = decay.transpose(1, 0, 2).reshape(c, nh * c)
    zeta_t = np.broadcast_to(zeta[:, None, :], (nh, dh, c)).reshape(nh * dh, c)
    xi_l = np.broadcast_to(xi.T[:, :, None], (c, nh, dh)).reshape(c, nh * dh)
    gamma_l = np.broadcast_to(gamma_chunk[:, None], (nh, dh)).reshape(1, nh * dh)
    scale = dh ** -0.5
    tables = (cos2 * scale, sin2 * scale, cos2.T, sin2.T, decay_l, zeta_t, xi_l, gamma_l)
    return tuple(jnp.asarray(np.ascontiguousarray(a, dtype=np.float32)) for a in tables)


def _ret(q, k_t, v, g, tables, gain, layer, batch):
    t = q.shape[0]
    nblk = t // batch // RET_ROWS
    row = pl.BlockSpec((RET_ROWS, RET_WIDTH), lambda b, j: (b * nblk + j, 0))
    row_t = pl.BlockSpec((RET_WIDTH, RET_ROWS), lambda b, j: (0, b * nblk + j))
    rot = pl.BlockSpec((RET_ROWS, RET_HEAD_DIM), lambda b, j: (j, 0))
    rot_t = pl.BlockSpec((RET_HEAD_DIM, RET_ROWS), lambda b, j: (0, j))
    const = lambda shape: pl.BlockSpec(shape, lambda b, j: (0, 0), pipeline_mode=pl.Buffered(1))
    return pl.pallas_call(
        _ret_kernel,
        grid=(batch, nblk),
        in_specs=[row, row_t, row, row, rot, rot, rot_t, rot_t,
                  const((RET_CHUNK, RET_WIDTH)), const((RET_WIDTH, RET_CHUNK)),
                  const((RET_CHUNK, RET_WIDTH)), const((1, RET_WIDTH)),
                  _resident((1, RET_WIDTH), layer)],
        out_specs=row,
        out_shape=jax.ShapeDtypeStruct((t, RET_WIDTH), BF16),
        scratch_shapes=[pltpu.VMEM((RET_HEAD_DIM, RET_WIDTH), F32),
                        pltpu.VMEM((RET_CHUNK, RET_WIDTH), F32)],
        compiler_params=_params(32, 2),
        name="ret",
    )(q, k_t, v, g, *tables, gain)


def _block_diag(w):
    depth, n, d, e = w.shape
    eye = jnp.eye(n, dtype=w.dtype)
    return jnp.einsum('lnde,nm->lndme', w, eye).reshape(depth, n * d, n * e)


def kernel(x, ffn1_norm, ffn1_w_gate, ffn1_w_up, ffn1_w_down, mix_norm, w_in, s5_lambda_re, s5_lambda_im, s5_log_step, s5_b_re, s5_b_im, s5_c_re, s5_c_im, s5_d, s5_w_glu, s5_b_glu, s5_out_norm, ret_out_norm, lru_conv_w, lru_conv_b, lru_w_a, lru_b_a, lru_w_x, lru_b_x, lru_lambda, lru_out_norm, w_out, ffn2_norm, ffn2_w_gate, ffn2_w_up, ffn2_w_down, final_norm):
    batch, seq, d = x.shape
    depth = w_in.shape[0]
    t = batch * seq
    assert d == D_MODEL and seq % max(S5_STATE_ROWS, LRU_ROWS, RET_ROWS) == 0
    assert t % max(FFN_ROWS, PROJ_ROWS, S5_OUT_ROWS) == 0

    row3 = lambda a: a.reshape(depth, 1, a.shape[-1])
    bf = lambda a: a.astype(BF16)

    ffn1_f32 = (ffn1_w_gate, ffn1_w_up, ffn1_w_down)
    ffn2_f32 = (ffn2_w_gate, ffn2_w_up, ffn2_w_down)
    ffn1_b = tuple(bf(w[0]) for w in ffn1_f32)
    w_in_b = bf(w_in)
    k_off = sum(IN_SECTIONS[:K_SECTION])
    w_k = lax.optimization_barrier(w_in[:, :, k_off:k_off + RET_WIDTH])
    w_k_t = bf(jnp.swapaxes(w_k, 1, 2))
    w_out_b = bf(w_out)
    s5_wst, s5_krev, s5_vt, s5_consts = _s5_prepare(s5_lambda_re, s5_lambda_im, s5_log_step,
                                                 s5_b_re, s5_b_im, s5_c_re, s5_c_im)
    s5_glu = bf(s5_w_glu)
    lru_wax = bf(jnp.concatenate([_block_diag(lru_w_a), _block_diag(lru_w_x)], axis=-1))
    lru_bax = row3(jnp.concatenate([lru_b_a, lru_b_x], axis=-1))
    tables = _ret_tables(seq)

    xt = x.reshape(t, d)
    for l in range(depth):
        xt = _ffn(xt, row3(ffn1_norm), ffn1_b, l)
        u, q, k_t, v, g, xl, gl = _inproj(xt, row3(mix_norm), w_in_b, w_k_t, l)
        uc, sp = _s5_state(u, s5_wst, s5_consts, l, batch)
        y_s5, ffn2_b = _s5_out(u, uc, sp, s5_krev, s5_vt, row3(s5_d), s5_glu, row3(s5_b_glu),
                               row3(s5_out_norm), ffn2_f32, l)
        y_ret = _ret(q, k_t, v, g, tables, row3(ret_out_norm), l, batch)
        y_lru, ffn1_b = _lru(xl, gl, lru_conv_w, row3(lru_conv_b), lru_wax, lru_bax,
                             row3(lru_lambda), row3(lru_out_norm), l, batch,
                             next_ffn_f32=ffn1_f32 if l + 1 < depth else None)
        last = final_norm.reshape(1, d) if l == depth - 1 else None
        xt = _ffn(xt, row3(ffn2_norm), ffn2_b, l, mix=(y_s5, y_ret, y_lru, w_out_b),
                  final_gain=last)
    return xt.reshape(batch, seq, d)
```

```python
import functools
import math

import jax
import jax.numpy as jnp
import numpy as np
from jax import lax
from jax.experimental import pallas as pl
from jax.experimental.pallas import tpu as pltpu

F32 = jnp.float32
BF16 = jnp.bfloat16

D_MODEL = 1024
D_FF = 2816
S5_WIDTH = 256
S5_GROUP_DIM = 16
S5_GROUPS = 16
S5_STATE = 64
S5_CH = S5_GROUPS * S5_STATE
RET_WIDTH = 512
RET_HEAD_DIM = 128
RET_HEADS = 4
RET_CHUNK = 128
LRU_WIDTH = 256
LRU_BLOCKS = 4
LRU_BLOCK_DIM = 64
CONV_WIDTH = 4
LRU_C = 8.0
ROPE_BASE = 10000.0
NORM_EPS = 1e-6
IN_SECTIONS = (S5_WIDTH, RET_WIDTH, RET_WIDTH, RET_WIDTH, RET_WIDTH, LRU_WIDTH, LRU_WIDTH)
IN_WIDTH = sum(IN_SECTIONS)

SUBLANES = 8
LANES = 128
MIB = 1024 * 1024

FFN_ROWS = 1024
FFN_CHUNK = 256
PROJ_ROWS = 1024
S5_STATE_ROWS = 4096
S5_OUT_ROWS = 4096
LRU_ROWS = 2048
LRU_UNROLL = 4
RET_ROWS = 1024
RET_UNROLL = 2


def _params(vmem_mib, n_axes):
    return pltpu.CompilerParams(
        dimension_semantics=("arbitrary",) * n_axes,
        vmem_limit_bytes=vmem_mib * MIB)


def _rms_rows(x):
    return x * lax.rsqrt(jnp.mean(x * x, axis=-1, keepdims=True) + NORM_EPS)


def _gelu_tanh(x):
    c = math.sqrt(2.0 / math.pi)
    return x * (0.5 * (1.0 + jnp.tanh(c * (x + 0.044715 * (x * x * x)))))


def _sigmoid(x):
    return 0.5 + 0.5 * jnp.tanh(0.5 * x)


def _silu(x):
    h = 0.5 * x
    return h + h * jnp.tanh(h)


def _dot(a, b):
    return jnp.dot(a, b, preferred_element_type=F32)


def _resident(shape, layer):
    nd = len(shape)
    return pl.BlockSpec((None,) + tuple(shape), lambda *_: (layer,) + (0,) * nd,
                        pipeline_mode=pl.Buffered(1))


def _ffn_kernel(*refs, with_outproj, with_final_norm):
    refs = list(refs)
    o_ref = refs.pop()
    x = refs.pop(0)[...]
    if with_outproj:
        ys_ref, yr_ref, yl_ref, wo_ref = refs[:4]
        refs = refs[4:]
        x = x + _dot(ys_ref[...], wo_ref[0:S5_WIDTH, :])
        x = x + _dot(yr_ref[...], wo_ref[S5_WIDTH:S5_WIDTH + RET_WIDTH, :])
        x = x + _dot(yl_ref[...], wo_ref[S5_WIDTH + RET_WIDTH:, :])
    g_ref, wg_ref, wu_ref, wd_ref = refs[:4]
    h = (_rms_rows(x) * g_ref[...]).astype(BF16)
    acc = jnp.zeros(x.shape, F32)
    for c in range(D_FF // FFN_CHUNK):
        sl = slice(c * FFN_CHUNK, (c + 1) * FFN_CHUNK)
        act = (_silu(_dot(h, wg_ref[:, sl])) * _dot(h, wu_ref[:, sl])).astype(BF16)
        acc = acc + _dot(act, wd_ref[sl, :])
    y = x + 0.5 * acc
    if with_final_norm:
        y = _rms_rows(y) * refs[4][...]
    o_ref[...] = y


def _ffn(x, gain, weights, layer, mix=None, final_gain=None):
    t = x.shape[0]
    spec = lambda w: pl.BlockSpec((FFN_ROWS, w), lambda i: (i, 0))
    whole = lambda a: (_resident(a.shape[1:], layer) if a.ndim == 3 else
                       pl.BlockSpec(a.shape, lambda i: (0, 0), pipeline_mode=pl.Buffered(1)))
    operands = [x]
    in_specs = [spec(D_MODEL)]
    if mix is not None:
        y_s5, y_ret, y_lru, w_out = mix
        operands += [y_s5, y_ret, y_lru, w_out]
        in_specs += [spec(S5_WIDTH), spec(RET_WIDTH), spec(LRU_WIDTH),
                     _resident((D_MODEL, D_MODEL), layer)]
    operands += [gain, *weights]
    in_specs += [_resident((1, D_MODEL), layer)] + [whole(w) for w in weights]
    if final_gain is not None:
        operands.append(final_gain)
        in_specs.append(pl.BlockSpec((1, D_MODEL), lambda i: (0, 0)))
    return pl.pallas_call(
        functools.partial(_ffn_kernel, with_outproj=mix is not None,
                          with_final_norm=final_gain is not None),
        grid=(t // FFN_ROWS,),
        in_specs=in_specs,
        out_specs=spec(D_MODEL),
        out_shape=jax.ShapeDtypeStruct(x.shape, F32),
        compiler_params=_params(56 if mix is not None else 50, 1),
        name="ffn_mix" if mix is not None else "ffn",
    )(*operands)


FFN_WEIGHT_SHAPES = ((D_MODEL, D_FF), (D_MODEL, D_FF), (D_FF, D_MODEL))


def _cast_plan(nsteps):
    slabs = [(r // nsteps, c) for r, c in FFN_WEIGHT_SHAPES]
    hbm = pl.BlockSpec(memory_space=pl.ANY)
    scratch = ([pltpu.VMEM((2,) + sl, F32) for sl in slabs] + [pltpu.VMEM(sl, BF16) for sl in slabs]
               + [pltpu.SemaphoreType.DMA((3, 3))])
    out_shape = [jax.ShapeDtypeStruct(shape, BF16) for shape in FFN_WEIGHT_SHAPES]
    return [hbm] * 3, [hbm] * 3, out_shape, scratch


def _cast_copies(layer, srcs, dsts, scratch):
    stage_in, stage_out, sem = scratch[:3], scratch[3:6], scratch[6]

    def rows(i, s):
        n = stage_out[i].shape[0]
        return pl.ds(pl.multiple_of(s * n, 16), n)

    fetch = lambda i, s: pltpu.make_async_copy(srcs[i].at[layer, rows(i, s), :],
                                               stage_in[i].at[s % 2], sem.at[i, s % 2])
    put = lambda i, s: pltpu.make_async_copy(stage_out[i], dsts[i].at[rows(i, s), :], sem.at[i, 2])
    return fetch, put, stage_in, stage_out


def _cast_begin(step, nsteps, layer, srcs, dsts, scratch):
    fetch, put, stage_in, stage_out = _cast_copies(layer, srcs, dsts, scratch)

    @pl.when(step == 0)
    def _():
        for i in range(3):
            fetch(i, 0).start()

    for i in range(3):
        fetch(i, step).wait()
        stage_out[i][...] = stage_in[i][step % 2].astype(BF16)
        put(i, step).start()

    @pl.when(step + 1 < nsteps)
    def _():
        for i in range(3):
            fetch(i, step + 1).start()


def _cast_end(step, layer, srcs, dsts, scratch):
    _, put, _, _ = _cast_copies(layer, srcs, dsts, scratch)
    for i in range(3):
        put(i, step).wait()


K_SECTION = 2


def _inproj_kernel(x_ref, g_ref, w_ref, wkt_ref, *out_refs):
    h = (_rms_rows(x_ref[...]) * g_ref[...]).astype(BF16)
    off = 0
    for n, (o_ref, width) in enumerate(zip(out_refs, IN_SECTIONS)):
        if n == K_SECTION:
            o_ref[...] = lax.dot_general(wkt_ref[...], h, (((1,), (1,)), ((), ())),
                                         preferred_element_type=F32)
        else:
            o_ref[...] = _dot(h, w_ref[:, off:off + width])
        off += width


def _inproj(x, gain, w_in, w_k_t, layer):
    t = x.shape[0]
    row = lambda w: pl.BlockSpec((PROJ_ROWS, w), lambda i: (i, 0))
    out_specs = [row(w) for w in IN_SECTIONS]
    out_shape = [jax.ShapeDtypeStruct((t, w), F32) for w in IN_SECTIONS]
    out_specs[K_SECTION] = pl.BlockSpec((RET_WIDTH, PROJ_ROWS), lambda i: (0, i))
    out_shape[K_SECTION] = jax.ShapeDtypeStruct((RET_WIDTH, t), F32)
    return pl.pallas_call(
        _inproj_kernel,
        grid=(t // PROJ_ROWS,),
        in_specs=[row(D_MODEL),
                  _resident((1, D_MODEL), layer),
                  _resident((D_MODEL, IN_WIDTH), layer),
                  _resident((RET_WIDTH, D_MODEL), layer)],
        out_specs=out_specs,
        out_shape=out_shape,
        compiler_params=_params(52, 1),
        name="inproj",
    )(x, gain, w_in, w_k_t)


S5_CHUNK = 8
S5_CAT = S5_CHUNK * S5_WIDTH
S5_CONST_ROWS = 8 * SUBLANES


def _cmul(a, b):
    return a[0] * b[0] - a[1] * b[1], a[0] * b[1] + a[1] * b[0]


def _s5_prep_kernel(lr_ref, li_ref, ls_ref, er_ref, ei_ref, ctr_ref, cti_ref, cst_ref,
                    wst_ref, kk_ref, vt_ref, c_ref):
    k = pl.program_id(1)
    lr = lr_ref[...]
    li = li_ref[...]
    step = jnp.exp(ls_ref[...])
    ar = lr * step
    ai = li * step
    mag = jnp.exp(ar)
    p1 = (mag * jnp.cos(ai), mag * jnp.sin(ai))
    nr = p1[0] - 1.0
    den = lr * lr + li * li
    f = ((nr * lr + p1[1] * li) / den, (p1[1] * lr - nr * li) / den)
    powers = [(jnp.ones_like(lr), jnp.zeros_like(lr)), p1]
    for _ in range(2, S5_CHUNK + 1):
        powers.append(_cmul(powers[-1], p1))

    def pick(n):
        out = powers[0]
        for i in range(1, S5_CHUNK + 1):
            out = (jnp.where(n == i, powers[i][0], out[0]), jnp.where(n == i, powers[i][1], out[1]))
        return out

    fp = _cmul(f, pick(k))
    er = er_ref[...]
    ei = ei_ref[...]
    w = jnp.concatenate([fp[0] * er - fp[1] * ei, fp[0] * ei + fp[1] * er], axis=1)
    wb = w.astype(BF16)
    wst_ref[...] = wb
    kk_ref[...] = _dot(wb, cst_ref[...]).astype(BF16)
    a, b = pick(k + 1)
    ctr = ctr_ref[...]
    cti = cti_ref[...]
    vt_ref[:, :S5_CH] = (ctr * a - cti * b).astype(BF16)
    vt_ref[:, S5_CH:] = (-(ctr * b) - cti * a).astype(BF16)

    @pl.when(k == 0)
    def _():
        q = [powers[S5_CHUNK]]
        for _ in range(1, SUBLANES):
            q.append(_cmul(q[-1], q[0]))
        row = lax.broadcasted_iota(jnp.int32, (SUBLANES, S5_CH), 0)
        zero = jnp.zeros((SUBLANES, S5_CH), F32)
        for s, d in enumerate((1, 2, 4)):
            for part in range(2):
                base = (2 * s + part) * SUBLANES
                c_ref[base:base + SUBLANES, :] = jnp.where(
                    row >= d, jnp.broadcast_to(q[d - 1][part], (SUBLANES, S5_CH)), zero)
        for part in range(2):
            tile = zero
            for r in range(SUBLANES):
                tile = jnp.where(row == r, jnp.broadcast_to(q[r][part], (SUBLANES, S5_CH)), tile)
            base = (6 + part) * SUBLANES
            c_ref[base:base + SUBLANES, :] = tile


def _s5_prepare(lam_re, lam_im, log_step, b_re, b_im, c_re, c_im):
    depth = lam_re.shape[0]
    def embed(a):
        n, m = a.shape[2:]
        same = np.equal.outer(np.repeat(np.arange(S5_GROUPS), n), np.arange(S5_GROUPS))
        tiled = jnp.broadcast_to(a.reshape(depth, S5_GROUPS * n, 1, m),
                                 (depth, S5_GROUPS * n, S5_GROUPS, m))
        return jnp.where(same[None, :, :, None], tiled, 0.0).reshape(
            depth, S5_GROUPS * n, S5_GROUPS * m)

    b_rows = lambda b: embed(jnp.swapaxes(b, 2, 3))
    c_rows = embed
    c_cols = lambda c: embed(jnp.swapaxes(c, 2, 3))
    c_stack = jnp.concatenate([c_cols(c_re), -c_cols(c_im)], axis=1).astype(BF16)
    vec = lambda a: a.reshape(depth, 1, S5_CH)
    ls = jnp.broadcast_to(log_step[:, :, None], (depth, S5_GROUPS, S5_STATE))
    vspec = pl.BlockSpec((None, 1, S5_CH), lambda l, k: (l, 0, 0))
    mspec = pl.BlockSpec((None, S5_WIDTH, S5_CH), lambda l, k: (l, 0, 0))
    reverse = lambda l, k: (l, S5_CHUNK - 1 - k, 0)
    return pl.pallas_call(
        _s5_prep_kernel,
        grid=(depth, S5_CHUNK),
        in_specs=[vspec, vspec, vspec, mspec, mspec, mspec, mspec,
                  pl.BlockSpec((None, 2 * S5_CH, S5_WIDTH), lambda l, k: (l, 0, 0))],
        out_specs=[pl.BlockSpec((None, S5_WIDTH, 2 * S5_CH), reverse),
                   pl.BlockSpec((None, S5_WIDTH, S5_WIDTH), reverse),
                   pl.BlockSpec((None, S5_WIDTH, 2 * S5_CH), lambda l, k: (l, k, 0)),
                   pl.BlockSpec((None, S5_CONST_ROWS, S5_CH), lambda l, k: (l, 0, 0))],
        out_shape=[jax.ShapeDtypeStruct((depth, S5_CAT, 2 * S5_CH), BF16),
                   jax.ShapeDtypeStruct((depth, S5_CAT, S5_WIDTH), BF16),
                   jax.ShapeDtypeStruct((depth, S5_CAT, 2 * S5_CH), BF16),
                   jax.ShapeDtypeStruct((depth, S5_CONST_ROWS, S5_CH), F32)],
        compiler_params=_params(40, 2),
        name="s5_prep",
    )(vec(lam_re), vec(lam_im), vec(ls), b_rows(b_re), b_rows(b_im), c_rows(c_re), c_rows(c_im),
      c_stack)


def _s5_state_kernel(ulo_ref, uhi_ref, wst_ref, k_ref, uc_ref, sp_ref, ds_scr, s_scr):
    j = pl.program_id(1)
    nch = ulo_ref.shape[0] // S5_CHUNK
    ncol = S5_CH // LANES
    cols = [(slice(cb * LANES, (cb + 1) * LANES),
             slice(S5_CH + cb * LANES, S5_CH + (cb + 1) * LANES)) for cb in range(ncol)]

    @pl.when(j == 0)
    def _():
        s_scr[0:SUBLANES, :] = jnp.zeros((SUBLANES, 2 * S5_CH), F32)

    @pl.when(j > 0)
    def _():
        s_scr[0:SUBLANES, :] = s_scr[nch:nch + SUBLANES, :]

    uc = jnp.concatenate([half[pl.ds(jj, nch, stride=S5_CHUNK), :]
                          for jj in range(S5_CHUNK) for half in (ulo_ref, uhi_ref)],
                         axis=1).astype(BF16)
    uc_ref[...] = uc
    ds_scr[...] = _dot(uc, wst_ref[...])

    def tile_body(i, carry):
        off = pl.multiple_of(i * SUBLANES, SUBLANES)
        loaded = [(ds_scr[pl.ds(off, SUBLANES), re], ds_scr[pl.ds(off, SUBLANES), im])
                  for re, im in cols]
        done = []
        for cb, (br, bi) in enumerate(loaded):
            re = cols[cb][0]
            for s, d in enumerate((1, 2, 4)):
                cr = k_ref[(2 * s) * SUBLANES:(2 * s + 1) * SUBLANES, re]
                ci = k_ref[(2 * s + 1) * SUBLANES:(2 * s + 2) * SUBLANES, re]
                rr = pltpu.roll(br, d, 0)
                ri = pltpu.roll(bi, d, 0)
                br, bi = br + (cr * rr - ci * ri), bi + (cr * ri + ci * rr)
            pr, pi = carry[cb], carry[ncol + cb]
            wr = k_ref[6 * SUBLANES:7 * SUBLANES, re]
            wi = k_ref[7 * SUBLANES:8 * SUBLANES, re]
            done.append((br + (wr * pr - wi * pi), bi + (wr * pi + wi * pr)))
        for (re, im), (sr, si) in zip(cols, done):
            s_scr[pl.ds(off + SUBLANES, SUBLANES), re] = sr
            s_scr[pl.ds(off + SUBLANES, SUBLANES), im] = si
        return tuple(last(sr) for sr, _ in done) + tuple(last(si) for _, si in done)

    last = lambda a: jnp.broadcast_to(a[SUBLANES - 1:SUBLANES, :], (SUBLANES, LANES))
    init = (tuple(last(s_scr[0:SUBLANES, re]) for re, _ in cols)
            + tuple(last(s_scr[0:SUBLANES, im]) for _, im in cols))
    lax.fori_loop(0, nch // SUBLANES, tile_body, init)

    sp_ref[...] = s_scr[SUBLANES - 1:SUBLANES - 1 + nch, :].astype(BF16)


def _s5_state(u, wst, consts, layer, batch):
    t = u.shape[0]
    nblk = t // batch // S5_STATE_ROWS
    nch = S5_STATE_ROWS // S5_CHUNK
    chunk_rows = pl.BlockSpec((nch, S5_CAT), lambda b, j: (b * nblk + j, 0))
    return pl.pallas_call(
        _s5_state_kernel,
        grid=(batch, nblk),
        in_specs=[pl.BlockSpec((S5_STATE_ROWS, LANES), lambda b, j: (b * nblk + j, 0)),
                  pl.BlockSpec((S5_STATE_ROWS, LANES), lambda b, j: (b * nblk + j, 1)),
                  _resident((S5_CAT, 2 * S5_CH), layer),
                  _resident((S5_CONST_ROWS, S5_CH), layer)],
        out_specs=[chunk_rows, chunk_rows],
        out_shape=[jax.ShapeDtypeStruct((t // S5_CHUNK, S5_CAT), BF16),
                   jax.ShapeDtypeStruct((t // S5_CHUNK, 2 * S5_CH), BF16)],
        scratch_shapes=[pltpu.VMEM((nch, 2 * S5_CH), F32),
                        pltpu.VMEM((nch + SUBLANES, 2 * S5_CH), F32)],
        compiler_params=_params(44, 2),
        name="s5_state",
    )(u, u, wst, consts)


def _s5_out_kernel(u_ref, uc_ref, sp_ref, krev_ref, vt_ref, d_ref, wglu_ref, bglu_ref, gain_ref,
                   o_ref, ylo_scr, yhi_scr):
    nch = uc_ref.shape[0]
    sp = sp_ref[...]
    for r in range(S5_CHUNK):
        cols = slice(r * S5_WIDTH, (r + 1) * S5_WIDTH)
        used = (r + 1) * S5_WIDTH
        y_r = _dot(uc_ref[:, :used], krev_ref[S5_CAT - used:, :])
        y_r = y_r + lax.dot_general(sp, vt_ref[cols, :], (((1,), (1,)), ((), ())),
                                    preferred_element_type=F32)
        ylo_scr[pl.ds(r, nch, stride=S5_CHUNK), :] = y_r[:, :LANES]
        yhi_scr[pl.ds(r, nch, stride=S5_CHUNK), :] = y_r[:, LANES:]
    y = jnp.concatenate([ylo_scr[...], yhi_scr[...]], axis=1) + d_ref[...] * u_ref[...]
    y = _gelu_tanh(y)
    y = y * _sigmoid(_dot(y.astype(BF16), wglu_ref[...]) + bglu_ref[...])
    o_ref[...] = (_rms_rows(y) * gain_ref[...]).astype(BF16)


def _s5_out(u, uc, sp, krev, vt, d_skip, w_glu, b_glu, gain, layer):
    t = u.shape[0]
    nch = S5_OUT_ROWS // S5_CHUNK
    row = pl.BlockSpec((S5_OUT_ROWS, S5_WIDTH), lambda i: (i, 0))
    chunk_rows = pl.BlockSpec((nch, S5_CAT), lambda i: (i, 0))
    return pl.pallas_call(
        _s5_out_kernel,
        grid=(t // S5_OUT_ROWS,),
        in_specs=[row, chunk_rows, chunk_rows,
                  _resident((S5_CAT, S5_WIDTH), layer),
                  _resident((S5_CAT, 2 * S5_CH), layer),
                  _resident((1, S5_WIDTH), layer),
                  _resident((S5_WIDTH, S5_WIDTH), layer),
                  _resident((1, S5_WIDTH), layer),
                  _resident((1, S5_WIDTH), layer)],
        out_specs=row,
        out_shape=jax.ShapeDtypeStruct((t, S5_WIDTH), BF16),
        scratch_shapes=[pltpu.VMEM((S5_OUT_ROWS, LANES), F32),
                        pltpu.VMEM((S5_OUT_ROWS, LANES), F32)],
        compiler_params=_params(48, 1),
        name="s5_out",
    )(u, uc, sp, krev, vt, d_skip, w_glu, b_glu, gain)


def _lru_kernel(x_ref, gate_ref, cw_ref, cb_ref, wax_ref, bax_ref, lam_ref, gain_ref, *refs,
                cast_layer):
    if cast_layer is None:
        o_ref, ext_scr, a_scr, b_scr, h_scr, carry_scr = refs
    else:
        srcs, o_ref, dsts = refs[:3], refs[3], refs[4:7]
        ext_scr, a_scr, b_scr, h_scr, carry_scr = refs[7:12]
        cast_scratch = refs[12:]
        step = pl.program_id(0) * pl.num_programs(1) + pl.program_id(1)
        _cast_begin(step, pl.num_programs(0) * pl.num_programs(1), cast_layer, srcs, dsts,
                    cast_scratch)
    j = pl.program_id(1)
    rows = x_ref.shape[0]
    pad = SUBLANES

    @pl.when(j == 0)
    def _():
        ext_scr[0:pad, :] = jnp.zeros((pad, LRU_WIDTH), F32)
        carry_scr[...] = jnp.zeros(carry_scr.shape, F32)

    @pl.when(j > 0)
    def _():
        ext_scr[0:pad, :] = ext_scr[rows:rows + pad, :]

    ext_scr[pad:rows + pad, :] = x_ref[...]
    xc = cb_ref[...] + cw_ref[CONV_WIDTH - 1:CONV_WIDTH, :] * ext_scr[pad:rows + pad, :]
    for lag in range(1, CONV_WIDTH):
        tap = cw_ref[CONV_WIDTH - 1 - lag:CONV_WIDTH - lag, :]
        xc = xc + tap * ext_scr[pad - lag:rows + pad - lag, :]

    gates = _sigmoid(_dot(xc.astype(BF16), wax_ref[...]) + bax_ref[...])
    r = gates[:, :LRU_WIDTH]
    i = gates[:, LRU_WIDTH:]
    z = -lam_ref[...]
    softplus = jnp.maximum(z, 0.0) + jnp.log1p(jnp.exp(-jnp.abs(z)))
    a = jnp.exp((-LRU_C) * r * softplus)
    a_scr[...] = a
    b_scr[...] = jnp.sqrt(1.0 - a * a) * (i * xc)

    row = lax.broadcasted_iota(jnp.int32, (SUBLANES, LRU_WIDTH), 0)

    def tile_body(t, prev):
        off = pl.multiple_of(t * SUBLANES, SUBLANES)
        at = a_scr[pl.ds(off, SUBLANES), :]
        bt = b_scr[pl.ds(off, SUBLANES), :]
        for d in (1, 2, 4):
            keep = row >= d
            ar = jnp.where(keep, pltpu.roll(at, d, 0), 1.0)
            br = jnp.where(keep, pltpu.roll(bt, d, 0), 0.0)
            bt = at * br + bt
            at = at * ar
        h = at * prev + bt
        h_scr[pl.ds(off, SUBLANES), :] = h
        return jnp.broadcast_to(h[SUBLANES - 1:SUBLANES, :], (SUBLANES, LRU_WIDTH))

    carry_scr[...] = lax.fori_loop(0, rows // SUBLANES, tile_body, carry_scr[...],
                                   unroll=LRU_UNROLL)

    y = h_scr[...] * _gelu_tanh(gate_ref[...])
    o_ref[...] = (_rms_rows(y) * gain_ref[...]).astype(BF16)
    if cast_layer is not None:
        _cast_end(step, cast_layer, srcs, dsts, cast_scratch)


def _lru(x, gate, conv_w, conv_b, wax, bax, lam, gain, layer, batch, next_ffn_f32=None):
    t = x.shape[0]
    nblk = t // batch // LRU_ROWS
    row = pl.BlockSpec((LRU_ROWS, LRU_WIDTH), lambda b, j: (b * nblk + j, 0))
    in_specs = [row, row,
                _resident((CONV_WIDTH, LRU_WIDTH), layer),
                _resident((1, LRU_WIDTH), layer),
                _resident((LRU_WIDTH, 2 * LRU_WIDTH), layer),
                _resident((1, 2 * LRU_WIDTH), layer),
                _resident((1, LRU_WIDTH), layer),
                _resident((1, LRU_WIDTH), layer)]
    out_specs = [row]
    out_shape = [jax.ShapeDtypeStruct((t, LRU_WIDTH), BF16)]
    scratch = [pltpu.VMEM((LRU_ROWS + SUBLANES, LRU_WIDTH), F32),
               pltpu.VMEM((LRU_ROWS, LRU_WIDTH), F32),
               pltpu.VMEM((LRU_ROWS, LRU_WIDTH), F32),
               pltpu.VMEM((LRU_ROWS, LRU_WIDTH), F32),
               pltpu.VMEM((SUBLANES, LRU_WIDTH), F32)]
    operands = [x, gate, conv_w, conv_b, wax, bax, lam, gain]
    if next_ffn_f32 is not None:
        cast_in, cast_out, cast_shape, cast_scratch = _cast_plan(batch * nblk)
        in_specs += cast_in
        out_specs += cast_out
        out_shape += cast_shape
        scratch += cast_scratch
        operands += list(next_ffn_f32)
    outs = pl.pallas_call(
        functools.partial(_lru_kernel,
                          cast_layer=None if next_ffn_f32 is None else layer + 1),
        grid=(batch, nblk),
        in_specs=in_specs,
        out_specs=out_specs,
        out_shape=out_shape,
        scratch_shapes=scratch,
        compiler_params=_params(52, 2),
        name="lru",
    )(*operands)
    return outs[0], tuple(outs[1:])


def _ret_kernel(q_ref, kt_ref, v_ref, g_ref, cos_ref, sin_ref, cost_ref, sint_ref, decay_ref,
                zetat_ref, xi_ref, gamma_ref, gain_ref, o_ref, state_scr, y_scr):
    j = pl.program_id(1)

    @pl.when(j == 0)
    def _():
        state_scr[...] = jnp.zeros(state_scr.shape, F32)

    half = RET_HEAD_DIM // 2

    def chunk(start):
        rows = pl.ds(start, RET_CHUNK)
        cs = cos_ref[rows, :]
        sn = sin_ref[rows, :]
        cst = cost_ref[:, rows]
        snt = sint_ref[:, rows]
        ssq = jnp.zeros((RET_CHUNK, 1), F32)
        for h in range(RET_HEADS):
            cols = slice(h * RET_HEAD_DIM, (h + 1) * RET_HEAD_DIM)
            qh = q_ref[rows, cols]
            kt = kt_ref[cols, rows]
            vh = v_ref[rows, cols].astype(BF16)
            qb = (qh * cs + pltpu.roll(qh, half, 1) * sn).astype(BF16)
            kr = kt * cst + jnp.concatenate([kt[half:], kt[:half]], axis=0) * snt
            scores = _dot(qb, kr.astype(BF16)) * decay_ref[:, cols]
            inner = _dot(scores.astype(BF16), vh)
            st = state_scr[:, cols]
            cross = _dot(qb, st.astype(BF16)) * xi_ref[:, cols]
            kz = (kr * zetat_ref[cols, :]).astype(BF16)
            state_scr[:, cols] = gamma_ref[:, cols] * st + _dot(kz, vh)
            yh = _silu(g_ref[rows, cols]) * _rms_rows(inner + cross)
            ssq = ssq + jnp.sum(yh * yh, axis=-1, keepdims=True)
            y_scr[:, cols] = yh
        inv = lax.rsqrt(ssq * (1.0 / RET_WIDTH) + NORM_EPS)
        o_ref[rows, :] = ((y_scr[...] * inv) * gain_ref[...]).astype(BF16)

    def body(i, carry):
        for c in range(RET_UNROLL):
            chunk(pl.multiple_of((i * RET_UNROLL + c) * RET_CHUNK, RET_CHUNK))
        return carry

    lax.fori_loop(0, q_ref.shape[0] // (RET_CHUNK * RET_UNROLL), body, 0)


def _ret_tables(seq):
    dh, c, nh = RET_HEAD_DIM, RET_CHUNK, RET_HEADS
    pos = np.arange(seq, dtype=np.float64)
    inv_freq = ROPE_BASE ** (-np.arange(0, dh, 2, dtype=np.float64) / dh)
    ang = pos[:, None] * inv_freq[None, :]
    cos = np.cos(ang)
    sin = np.sin(ang)
    cos2 = np.concatenate([cos, cos], axis=-1)
    sin2 = np.concatenate([-sin, sin], axis=-1)
    log_gamma = np.log1p(-np.exp2(-5.0 - np.arange(nh, dtype=np.float64)))
    idx = np.arange(c, dtype=np.float64)
    diff = idx[:, None] - idx[None, :]
    decay = np.where(diff[None] >= 0,
                     np.exp(np.maximum(diff, 0.0)[None] * log_gamma[:, None, None]), 0.0)
    zeta = np.exp((c - 1.0 - idx)[None] * log_gamma[:, None])
    xi = np.exp((idx + 1.0)[None] * log_gamma[:, None])
    gamma_chunk = np.exp(c * log_gamma)
    decay_l = decay.transpose(1, 0, 2).reshape(c, nh * c)
    zeta_t = np.broadcast_to(zeta[:, None, :], (nh, dh, c)).reshape(nh * dh, c)
    xi_l = np.broadcast_to(xi.T[:, :, None], (c, nh, dh)).reshape(c, nh * dh)
    gamma_l = np.broadcast_to(gamma_chunk[:, None], (nh, dh)).reshape(1, nh * dh)
    scale = dh ** -0.5
    tables = (cos2 * scale, sin2 * scale, cos2.T, sin2.T, decay_l, zeta_t, xi_l, gamma_l)
    return tuple(jnp.asarray(np.ascontiguousarray(a, dtype=np.float32)) for a in tables)


def _ret(q, k_t, v, g, tables, gain, layer, batch):
    t = q.shape[0]
    nblk = t // batch // RET_ROWS
    row = pl.BlockSpec((RET_ROWS, RET_WIDTH), lambda b, j: (b * nblk + j, 0))
    row_t = pl.BlockSpec((RET_WIDTH, RET_ROWS), lambda b, j: (0, b * nblk + j))
    rot = pl.BlockSpec((RET_ROWS, RET_HEAD_DIM), lambda b, j: (j, 0))
    rot_t = pl.BlockSpec((RET_HEAD_DIM, RET_ROWS), lambda b, j: (0, j))
    const = lambda shape: pl.BlockSpec(shape, lambda b, j: (0, 0), pipeline_mode=pl.Buffered(1))
    return pl.pallas_call(
        _ret_kernel,
        grid=(batch, nblk),
        in_specs=[row, row_t, row, row, rot, rot, rot_t, rot_t,
                  const((RET_CHUNK, RET_WIDTH)), const((RET_WIDTH, RET_CHUNK)),
                  const((RET_CHUNK, RET_WIDTH)), const((1, RET_WIDTH)),
                  _resident((1, RET_WIDTH), layer)],
        out_specs=row,
        out_shape=jax.ShapeDtypeStruct((t, RET_WIDTH), BF16),
        scratch_shapes=[pltpu.VMEM((RET_HEAD_DIM, RET_WIDTH), F32),
                        pltpu.VMEM((RET_CHUNK, RET_WIDTH), F32)],
        compiler_params=_params(32, 2),
        name="ret",
    )(q, k_t, v, g, *tables, gain)


def _block_diag(w):
    depth, n, d, e = w.shape
    eye = jnp.eye(n, dtype=w.dtype)
    return jnp.einsum('lnde,nm->lndme', w, eye).reshape(depth, n * d, n * e)


def kernel(x, ffn1_norm, ffn1_w_gate, ffn1_w_up, ffn1_w_down, mix_norm, w_in, s5_lambda_re, s5_lambda_im, s5_log_step, s5_b_re, s5_b_im, s5_c_re, s5_c_im, s5_d, s5_w_glu, s5_b_glu, s5_out_norm, ret_out_norm, lru_conv_w, lru_conv_b, lru_w_a, lru_b_a, lru_w_x, lru_b_x, lru_lambda, lru_out_norm, w_out, ffn2_norm, ffn2_w_gate, ffn2_w_up, ffn2_w_down, final_norm):
    batch, seq, d = x.shape
    depth = w_in.shape[0]
    t = batch * seq
    assert d == D_MODEL and seq % max(S5_STATE_ROWS, LRU_ROWS, RET_ROWS) == 0
    assert t % max(FFN_ROWS, PROJ_ROWS, S5_OUT_ROWS) == 0

    row3 = lambda a: a.reshape(depth, 1, a.shape[-1])
    bf = lambda a: a.astype(BF16)

    ffn1_f32 = (ffn1_w_gate, ffn1_w_up, ffn1_w_down)
    ffn2_f32 = (ffn2_w_gate, ffn2_w_up, ffn2_w_down)
    ffn1_b = tuple(bf(w[0]) for w in ffn1_f32)
    ffn2_b = tuple(bf(w) for w in ffn2_f32)
    w_in_b = bf(w_in)
    k_off = sum(IN_SECTIONS[:K_SECTION])
    w_k = lax.optimization_barrier(w_in[:, :, k_off:k_off + RET_WIDTH])
    w_k_t = bf(jnp.swapaxes(w_k, 1, 2))
    w_out_b = bf(w_out)
    s5_wst, s5_krev, s5_vt, s5_consts = _s5_prepare(s5_lambda_re, s5_lambda_im, s5_log_step,
                                                 s5_b_re, s5_b_im, s5_c_re, s5_c_im)
    s5_glu = bf(s5_w_glu)
    lru_wax = bf(jnp.concatenate([_block_diag(lru_w_a), _block_diag(lru_w_x)], axis=-1))
    lru_bax = row3(jnp.concatenate([lru_b_a, lru_b_x], axis=-1))
    tables = _ret_tables(seq)

    xt = x.reshape(t, d)
    for l in range(depth):
        xt = _ffn(xt, row3(ffn1_norm), ffn1_b, l)
        u, q, k_t, v, g, xl, gl = _inproj(xt, row3(mix_norm), w_in_b, w_k_t, l)
        uc, sp = _s5_state(u, s5_wst, s5_consts, l, batch)
        y_s5 = _s5_out(u, uc, sp, s5_krev, s5_vt, row3(s5_d), s5_glu, row3(s5_b_glu),
                       row3(s5_out_norm), l)
        y_ret = _ret(q, k_t, v, g, tables, row3(ret_out_norm), l, batch)
        y_lru, ffn1_b = _lru(xl, gl, lru_conv_w, row3(lru_conv_b), lru_wax, lru_bax,
                             row3(lru_lambda), row3(lru_out_norm), l, batch,
                             next_ffn_f32=ffn1_f32 if l + 1 < depth else None)
        last = final_norm.reshape(1, d) if l == depth - 1 else None
        xt = _ffn(xt, row3(ffn2_norm), ffn2_b, l, mix=(y_s5, y_ret, y_lru, w_out_b),
                  final_gain=last)
    return xt.reshape(batch, seq, d)
```
